```python
import math, functools
import jax, jax.numpy as jnp
from jax import lax
import numpy as np

D_MODEL = 2048
BATCH = 2
SEQ = 4096
DEPTH = 4
DEC_BATCH = 8
DEC_SEQ = 8
PAST_LEN = 16384
PAGE_SIZE = 128

N_HEADS = D_MODEL // 256
HEAD_DIM = 128
KV_HEADS = 2
ATTN_W = N_HEADS * HEAD_DIM
IDX_HEADS = 8
IDX_DIM = 64
INDEX_TOPK = 256
Q_BLOCK = 128
LRU_WIDTH = D_MODEL // 2
LRU_BLOCKS = 8
LRU_BLOCK_W = LRU_WIDTH // LRU_BLOCKS
CONV_W = 4
LRU_C = 8.0
MEM_TOKENS = 256
MEM_HEADS = 4
MEM_HEAD_DIM = D_MODEL // 8
MEM_W = MEM_HEADS * MEM_HEAD_DIM
BRANCH_W = D_MODEL // 2
N_BRANCH = 3
ROPE_THETA = 10000.0
NORM_EPS = 1e-6
IN_SIZES = (ATTN_W, KV_HEADS * HEAD_DIM, KV_HEADS * HEAD_DIM, IDX_HEADS * IDX_DIM, IDX_DIM, IDX_HEADS,
            ATTN_W, LRU_WIDTH, LRU_WIDTH, MEM_W, MEM_W, N_BRANCH * D_MODEL)
D_IN = sum(IN_SIZES)

kernel_name = 'hybrid_dsa_rglru_memory_step'


def rmsnorm(x, g):
    xf = x.astype(jnp.float32)
    y = xf * lax.rsqrt(jnp.mean(xf * xf, axis=-1, keepdims=True) + NORM_EPS)
    return (y * g.astype(jnp.float32)).astype(x.dtype)


def rope(x, pos):
    half = x.shape[-1] // 2
    freq = ROPE_THETA ** (-jnp.arange(half, dtype=jnp.float32) / half)
    ang = pos.astype(jnp.float32)[:, None] * freq[None, :]
    cos = jnp.cos(ang)[None, :, None, :]
    sin = jnp.sin(ang)[None, :, None, :]
    xf = x.astype(jnp.float32)
    x1, x2 = xf[..., :half], xf[..., half:]
    return jnp.concatenate([x1 * cos - x2 * sin, x2 * cos + x1 * sin], axis=-1).astype(x.dtype)


def gather_rows(a, idx):
    return jax.vmap(lambda ab, ib: ab[ib])(a, idx)


def dsa_select(qi, wi, ki, qpos, kpos, topk):
    s = jnp.einsum('bqhd,bsd->bqhs', qi, ki, preferred_element_type=jnp.float32) * (IDX_DIM ** -0.5)
    score = jnp.einsum('bqhs,bqh->bqs', jax.nn.relu(s), wi.astype(jnp.float32))
    admissible = kpos[None, :] <= qpos[:, None]
    score = jnp.where(admissible[None], score, -jnp.inf)
    _, idx = lax.top_k(score, topk)
    valid = idx <= qpos[None, :, None]
    return idx, valid


def sparse_attend(q, ks, vs, valid):
    bq, nq = q.shape[:2]
    qg = q.reshape(bq, nq, KV_HEADS, N_HEADS // KV_HEADS, HEAD_DIM)
    s = jnp.einsum('bqngd,bqsnd->bqngs', qg, ks, preferred_element_type=jnp.float32) * (HEAD_DIM ** -0.5)
    s = jnp.where(valid[:, :, None, None, :], s, -jnp.inf)
    p = jax.nn.softmax(s, axis=-1).astype(vs.dtype)
    o = jnp.einsum('bqngs,bqsnd->bqngd', p, vs)
    return o.reshape(bq, nq, ATTN_W)


def prompt_attention(q, k, v, qi, ki, wi):
    bq, S = q.shape[:2]
    topk = min(INDEX_TOPK, S // 4)
    kpos = jnp.arange(S)

    def block(i):
        start = i * Q_BLOCK
        sl = lambda a: lax.dynamic_slice_in_dim(a, start, Q_BLOCK, axis=1)
        qpos = start + jnp.arange(Q_BLOCK)
        idx, valid = dsa_select(sl(qi), sl(wi), ki, qpos, kpos, topk)
        return sparse_attend(sl(q), gather_rows(k, idx), gather_rows(v, idx), valid)

    out = lax.map(block, jnp.arange(S // Q_BLOCK))
    return jnp.swapaxes(out, 0, 1).reshape(bq, S, ATTN_W)


def fetch_paged(pool, page_table, new, idx, past):
    bq, nq, nk = idx.shape
    page = pool.shape[1]
    ip = jnp.minimum(idx, past - 1)
    phys = jnp.take_along_axis(page_table, (ip // page).reshape(bq, nq * nk), axis=1).reshape(bq, nq, nk)
    from_past = pool[phys, ip % page]
    from_new = gather_rows(new, jnp.clip(idx - past, 0, new.shape[1] - 1))
    is_past = (idx < past).reshape(idx.shape + (1,) * (new.ndim - 2))
    return jnp.where(is_past, from_past, from_new)


def sample_attention(q, k, v, qi, ki, wi, pool_k, pool_v, pool_ik, page_table):
    bq, T = q.shape[:2]
    past = page_table.shape[1] * pool_k.shape[1]
    ki_past = pool_ik[page_table].reshape(bq, past, IDX_DIM)
    ki_all = jnp.concatenate([ki_past, ki.astype(ki_past.dtype)], axis=1)
    L = past + T
    topk = min(INDEX_TOPK, L // 4)
    qpos = past + jnp.arange(T)
    idx, valid = dsa_select(qi, wi, ki_all, qpos, jnp.arange(L), topk)
    ks = fetch_paged(pool_k, page_table, k.astype(pool_k.dtype), idx, past)
    vs = fetch_paged(pool_v, page_table, v.astype(pool_v.dtype), idx, past)
    return sparse_attend(q, ks, vs, valid)


def rglru_branch(xr, conv_prev, h_prev, conv_w, conv_b, w_r, b_r, w_i, b_i, lam):
    bq, T, W = xr.shape
    xp = jnp.concatenate([conv_prev.astype(xr.dtype), xr], axis=1)
    xc = conv_b + sum(xp[:, j:j + T] * conv_w[j] for j in range(CONV_W))
    xb = xc.reshape(bq, T, LRU_BLOCKS, LRU_BLOCK_W)
    r = jax.nn.sigmoid(jnp.einsum('btnc,ncd->btnd', xb, w_r).reshape(bq, T, W) + b_r)
    i = jax.nn.sigmoid(jnp.einsum('btnc,ncd->btnd', xb, w_i).reshape(bq, T, W) + b_i)
    log_a = -LRU_C * r.astype(jnp.float32) * jax.nn.softplus(-lam.astype(jnp.float32))
    a = jnp.exp(log_a)
    b = jnp.sqrt(-jnp.expm1(2.0 * log_a)) * (i * xc).astype(jnp.float32)

    def combine(e1, e2):
        return e1[0] * e2[0], e2[0] * e1[1] + e2[1]

    a_cum, b_cum = lax.associative_scan(combine, (a, b), axis=1)
    h = a_cum * h_prev.astype(jnp.float32)[:, None, :] + b_cum
    return h.astype(xr.dtype), h[:, -1].astype(xr.dtype), xp[:, T:]


def memory_kv(mem, g, w):
    m = rmsnorm(mem, g) @ w
    mk, mv = jnp.split(m, 2, axis=-1)
    shp = mem.shape[:2] + (MEM_HEADS, MEM_HEAD_DIM)
    return mk.reshape(shp), mv.reshape(shp)


def memory_attend(qm, mk, mv):
    s = jnp.einsum('bthd,bmhd->bthm', qm, mk, preferred_element_type=jnp.float32) * (MEM_HEAD_DIM ** -0.5)
    p = jax.nn.softmax(s, axis=-1).astype(mv.dtype)
    o = jnp.einsum('bthm,bmhd->bthd', p, mv)
    return o.reshape(qm.shape[0], qm.shape[1], MEM_W)


def mixer_layer(x, pos, attend, conv_prev, h_prev, mem_k, mem_v,
                norm_g, w_in, conv_w, conv_b, w_r, b_r, w_i, b_i, lam, w_branch, w_out):
    bq, T, _ = x.shape
    xn = rmsnorm(x, norm_g)
    z = xn @ w_in
    cuts = np.cumsum(IN_SIZES)[:-1].tolist()
    q, k, v, qi, ki, wi, ga, xr, gr, qm, gm, gl = jnp.split(z, cuts, axis=-1)
    q = rope(q.reshape(bq, T, N_HEADS, HEAD_DIM), pos)
    k = rope(k.reshape(bq, T, KV_HEADS, HEAD_DIM), pos)
    v = v.reshape(bq, T, KV_HEADS, HEAD_DIM)
    qi = rope(qi.reshape(bq, T, IDX_HEADS, IDX_DIM), pos)
    ki = rope(ki.reshape(bq, T, 1, IDX_DIM), pos)[:, :, 0]
    wi = wi * (IDX_HEADS ** -0.5)
    o_att = attend(q, k, v, qi, ki, wi)
    o_lru, h_last, conv_new = rglru_branch(xr, conv_prev, h_prev, conv_w, conv_b, w_r, b_r, w_i, b_i, lam)
    o_mem = memory_attend(qm.reshape(bq, T, MEM_HEADS, MEM_HEAD_DIM), mem_k.astype(x.dtype), mem_v.astype(x.dtype))
    branches = jnp.stack([o_att * jax.nn.silu(ga), o_lru * jax.nn.silu(gr), o_mem * jax.nn.silu(gm)], axis=2)
    proj = jnp.einsum('btnw,nwd->btnd', branches, w_branch)
    gates = jax.nn.sigmoid(gl.reshape(bq, T, N_BRANCH, D_MODEL))
    merged = jnp.sum(gates * proj, axis=2)
    return x + merged @ w_out, (k, v, ki, conv_new, h_last)


def setup_inputs(seed: int = 0) -> dict:
    key = jax.random.key(seed)
    ks = jax.random.split(key, 26)
    f32 = jnp.float32
    n_pages = PAST_LEN // PAGE_SIZE
    n_pool = (5 * DEC_BATCH * n_pages + 3) // 4

    def nrm(k, shape, scale):
        return jax.random.normal(k, shape, f32) * scale

    a8 = jax.random.uniform(ks[19], (DEPTH, LRU_WIDTH), f32, 0.9, 0.999)
    a = a8 ** (1.0 / LRU_C)
    perm = jax.random.permutation(ks[10], n_pool)
    return {
        'x_prompt': nrm(ks[0], (BATCH, SEQ, D_MODEL), 1.0),
        'x_sample': nrm(ks[1], (DEC_BATCH, DEC_SEQ, D_MODEL), 1.0),
        'mem_prompt': nrm(ks[2], (BATCH, MEM_TOKENS, D_MODEL), 1.0),
        'cache_k': nrm(ks[3], (DEPTH, n_pool, PAGE_SIZE, KV_HEADS, HEAD_DIM), 1.0),
        'cache_v': nrm(ks[4], (DEPTH, n_pool, PAGE_SIZE, KV_HEADS, HEAD_DIM), 1.0),
        'cache_idx_k': nrm(ks[5], (DEPTH, n_pool, PAGE_SIZE, IDX_DIM), 1.0),
        'cache_mem_k': nrm(ks[6], (DEPTH, DEC_BATCH, MEM_TOKENS, MEM_HEADS, MEM_HEAD_DIM), 1.0),
        'cache_mem_v': nrm(ks[7], (DEPTH, DEC_BATCH, MEM_TOKENS, MEM_HEADS, MEM_HEAD_DIM), 1.0),
        'state_conv': nrm(ks[8], (DEPTH, DEC_BATCH, CONV_W - 1, LRU_WIDTH), 1.0),
        'state_h': nrm(ks[9], (DEPTH, DEC_BATCH, LRU_WIDTH), 0.5),
        'page_table': perm[:DEC_BATCH * n_pages].reshape(DEC_BATCH, n_pages).astype(jnp.int32),
        'norm_g': 1.0 + nrm(ks[11], (DEPTH, D_MODEL), 0.05),
        'w_in': nrm(ks[12], (DEPTH, D_MODEL, D_IN), D_MODEL ** -0.5),
        'conv_w': nrm(ks[13], (DEPTH, CONV_W, LRU_WIDTH), CONV_W ** -0.5),
        'conv_b': nrm(ks[14], (DEPTH, LRU_WIDTH), 0.01),
        'w_rgate': nrm(ks[15], (DEPTH, LRU_BLOCKS, LRU_BLOCK_W, LRU_BLOCK_W), LRU_BLOCK_W ** -0.5),
        'b_rgate': nrm(ks[16], (DEPTH, LRU_WIDTH), 0.01),
        'w_igate': nrm(ks[17], (DEPTH, LRU_BLOCKS, LRU_BLOCK_W, LRU_BLOCK_W), LRU_BLOCK_W ** -0.5),
        'b_igate': nrm(ks[18], (DEPTH, LRU_WIDTH), 0.01),
        'lru_lambda': jnp.log(a) - jnp.log1p(-a),
        'mem_norm_g': 1.0 + nrm(ks[20], (DEPTH, D_MODEL), 0.05),
        'w_mem_kv': nrm(ks[21], (DEPTH, D_MODEL, 2 * MEM_W), D_MODEL ** -0.5),
        'w_branch': nrm(ks[22], (DEPTH, N_BRANCH, BRANCH_W, D_MODEL), BRANCH_W ** -0.5),
        'w_out': nrm(ks[23], (DEPTH, D_MODEL, D_MODEL), D_MODEL ** -0.5),
        'final_norm_g': 1.0 + nrm(ks[24], (D_MODEL,), 0.05),
    }


def reference(x_prompt, x_sample, mem_prompt, cache_k, cache_v, cache_idx_k, cache_mem_k, cache_mem_v,
              state_conv, state_h, page_table, norm_g, w_in, conv_w, conv_b, w_rgate, b_rgate,
              w_igate, b_igate, lru_lambda, mem_norm_g, w_mem_kv, w_branch, w_out, final_norm_g):
    bp, S, _ = x_prompt.shape
    bs, T, _ = x_sample.shape
    past = page_table.shape[1] * cache_k.shape[2]
    pos_p = jnp.arange(S)
    pos_s = past + jnp.arange(T)
    conv0 = jnp.zeros((bp, CONV_W - 1, LRU_WIDTH), x_prompt.dtype)
    h0 = jnp.zeros((bp, LRU_WIDTH), x_prompt.dtype)
    xp, xs = x_prompt, x_sample
    st_p, st_s, mk_list, mv_list = [], [], [], []
    for l in range(DEPTH):
        lw = (norm_g[l], w_in[l], conv_w[l], conv_b[l], w_rgate[l], b_rgate[l], w_igate[l], b_igate[l],
              lru_lambda[l], w_branch[l], w_out[l])
        mk, mv = memory_kv(mem_prompt, mem_norm_g[l], w_mem_kv[l])
        xp, sp = mixer_layer(xp, pos_p, prompt_attention, conv0, h0, mk, mv, *lw)
        attend_s = functools.partial(sample_attention, pool_k=cache_k[l], pool_v=cache_v[l],
                                     pool_ik=cache_idx_k[l], page_table=page_table)
        xs, ss = mixer_layer(xs, pos_s, attend_s, state_conv[l], state_h[l], cache_mem_k[l], cache_mem_v[l], *lw)
        st_p.append(sp)
        st_s.append(ss)
        mk_list.append(mk)
        mv_list.append(mv)
    y_prompt = rmsnorm(xp, final_norm_g)
    y_sample = rmsnorm(xs, final_norm_g)
    k_p, v_p, ik_p, conv_p, h_p = [jnp.stack(t) for t in zip(*st_p)]
    k_s, v_s, ik_s, conv_s, h_s = [jnp.stack(t) for t in zip(*st_s)]
    mem_k_p = jnp.stack(mk_list)
    mem_v_p = jnp.stack(mv_list)
    return (y_prompt, y_sample, k_p, v_p, ik_p, conv_p, h_p, mem_k_p, mem_v_p, k_s, v_s, ik_s, conv_s, h_s)
```

```python
import functools
import math

import jax
import jax.numpy as jnp
from jax import lax
from jax.experimental import pallas as pl
from jax.experimental.pallas import tpu as pltpu

F32 = jnp.float32
BF16 = jnp.bfloat16
I32 = jnp.int32

N_HEADS = 8
HEAD_DIM = 128
KV_HEADS = 2
HEADS_PER_KV = N_HEADS // KV_HEADS
IDX_HEADS = 8
IDX_DIM = 64
INDEX_TOPK = 256
LRU_BLOCKS = 8
CONV_W = 4
LRU_C = 8.0
MEM_HEADS = 4
N_BRANCH = 3
ROPE_THETA = 10000.0
NORM_EPS = 1e-6

LANES = 128
SUBLANES = 8
VMEM_LIMIT = 48 * 1024 * 1024

NEG_BIG = -1e30
INT_MIN = -(2 ** 31)


def _cparams(n_axes):
    return pltpu.CompilerParams(dimension_semantics=("arbitrary",) * n_axes,
                                vmem_limit_bytes=VMEM_LIMIT)


def _sigmoid(x):
    return 1.0 / (1.0 + jnp.exp(-x))


def _silu(x):
    return x * _sigmoid(x)


def _dot_nt(a, b):
    return lax.dot_general(a, b, (((1,), (1,)), ((), ())), preferred_element_type=F32)


def _proj_attn_kernel(x_ref, g_ref, w_ref, cq_ref, sq_ref, ci_ref, si_ref,
                      xn_ref, q_ref, k_ref, v_ref, qi_ref, ki_ref, wi_ref,
                      kb_ref, vb_ref, kib_ref, *, d_model):
    attn_w = N_HEADS * HEAD_DIM
    kv_w = KV_HEADS * HEAD_DIM
    x = x_ref[...]
    ms = jnp.mean(x * x, axis=-1, keepdims=True)
    xn = (x * lax.rsqrt(ms + NORM_EPS)) * g_ref[...]
    xnb = xn.astype(BF16)
    xn_ref[...] = xnb
    z = jnp.dot(xnb, w_ref[...], preferred_element_type=F32)

    cq, sq, ci, si = cq_ref[...], sq_ref[...], ci_ref[...], si_ref[...]
    lane = lax.broadcasted_iota(I32, cq.shape, 1)
    first_half = (lane & (IDX_DIM // 2)) == 0

    def rope128(t):
        return t * cq + pltpu.roll(t, HEAD_DIM // 2, 1) * sq

    def rope64(t):
        rot = jnp.where(first_half, pltpu.roll(t, LANES - IDX_DIM // 2, 1),
                        pltpu.roll(t, IDX_DIM // 2, 1))
        return t * ci + rot * si

    for h in range(N_HEADS):
        sl = slice(h * HEAD_DIM, (h + 1) * HEAD_DIM)
        q_ref[:, sl] = rope128(z[:, sl]).astype(BF16)
    for h in range(KV_HEADS):
        sl = slice(h * HEAD_DIM, (h + 1) * HEAD_DIM)
        kr = rope128(z[:, attn_w + h * HEAD_DIM: attn_w + (h + 1) * HEAD_DIM])
        k_ref[:, sl] = kr
        kb_ref[:, sl] = kr.astype(BF16)
    v = z[:, attn_w + kv_w: attn_w + 2 * kv_w]
    v_ref[...] = v
    vb_ref[...] = v.astype(BF16)
    off = attn_w + 2 * kv_w
    for c in range(IDX_HEADS * IDX_DIM // LANES):
        sl = slice(c * LANES, (c + 1) * LANES)
        qi_ref[:, sl] = rope64(z[:, off + c * LANES: off + (c + 1) * LANES]).astype(BF16)
    off += IDX_HEADS * IDX_DIM
    kir = rope64(z[:, off: off + LANES])[:, :IDX_DIM]
    ki_ref[...] = kir
    kib_ref[...] = kir.astype(BF16)
    off += LANES
    wi_ref[...] = z[:, off: off + IDX_HEADS] * (IDX_HEADS ** -0.5)


def _proj_attn(x, g, w_a, tabs, tm):
    m, d = x.shape
    na = w_a.shape[1]
    attn_w, kv_w, qi_w = N_HEADS * HEAD_DIM, KV_HEADS * HEAD_DIM, IDX_HEADS * IDX_DIM
    row = lambda i: (i, 0)
    const = lambda i: (0, 0)
    outs = [((m, d), BF16), ((m, attn_w), BF16), ((m, kv_w), F32), ((m, kv_w), F32),
            ((m, qi_w), BF16), ((m, IDX_DIM), F32), ((m, IDX_HEADS), F32),
            ((m, kv_w), BF16), ((m, kv_w), BF16), ((m, IDX_DIM), BF16)]
    return pl.pallas_call(
        functools.partial(_proj_attn_kernel, d_model=d),
        grid=(m // tm,),
        in_specs=[pl.BlockSpec((tm, d), row), pl.BlockSpec((1, d), const), pl.BlockSpec((d, na), const)]
                 + [pl.BlockSpec((tm, LANES), row)] * 4,
        out_specs=[pl.BlockSpec((tm, s[1]), row) for s, _ in outs],
        out_shape=[jax.ShapeDtypeStruct(s, dt) for s, dt in outs],
        compiler_params=_cparams(1),
        name="proj_attn",
    )(x, g, w_a, *tabs)


def _mm_kernel(x_ref, w_ref, o_ref):
    o_ref[...] = jnp.dot(x_ref[...], w_ref[...], preferred_element_type=F32).astype(o_ref.dtype)


def _mm_res_kernel(x_ref, w_ref, r_ref, o_ref):
    o_ref[...] = r_ref[...] + jnp.dot(x_ref[...], w_ref[...], preferred_element_type=F32)


def _matmul(x, w, tm, tn, residual=None):
    m, k = x.shape
    n = w.shape[1]
    in_specs = [pl.BlockSpec((tm, k), lambda j, i: (i, 0)), pl.BlockSpec((k, tn), lambda j, i: (0, j))]
    args = [x, w]
    if residual is not None:
        in_specs.append(pl.BlockSpec((tm, tn), lambda j, i: (i, j)))
        args.append(residual)
    return pl.pallas_call(
        _mm_kernel if residual is None else _mm_res_kernel,
        grid=(n // tn, m // tm),
        in_specs=in_specs,
        out_specs=pl.BlockSpec((tm, tn), lambda j, i: (i, j)),
        out_shape=jax.ShapeDtypeStruct((m, n), F32),
        compiler_params=_cparams(2),
        name="matmul" if residual is None else "matmul_residual",
    )(*args)


def _norm_mm_kernel(x_ref, g_ref, w_ref, o_ref):
    x = x_ref[...]
    ms = jnp.mean(x * x, axis=-1, keepdims=True)
    xn = ((x * lax.rsqrt(ms + NORM_EPS)) * g_ref[...]).astype(BF16)
    o_ref[...] = jnp.dot(xn, w_ref[...], preferred_element_type=F32)


def _norm_matmul(x, g, w, tm, tn):
    m, k = x.shape
    n = w.shape[1]
    return pl.pallas_call(
        _norm_mm_kernel,
        grid=(n // tn, m // tm),
        in_specs=[pl.BlockSpec((tm, k), lambda j, i: (i, 0)), pl.BlockSpec((1, k), lambda j, i: (0, 0)),
                  pl.BlockSpec((k, tn), lambda j, i: (0, j))],
        out_specs=pl.BlockSpec((tm, tn), lambda j, i: (i, j)),
        out_shape=jax.ShapeDtypeStruct((m, n), F32),
        compiler_params=_cparams(2),
        name="norm_matmul",
    )(x, g, w)


def _rmsnorm_kernel(x_ref, g_ref, o_ref):
    x = x_ref[...]
    ms = jnp.mean(x * x, axis=-1, keepdims=True)
    o_ref[...] = (x * lax.rsqrt(ms + NORM_EPS)) * g_ref[...]


def _rmsnorm(x, g, tm):
    m, d = x.shape
    return pl.pallas_call(
        _rmsnorm_kernel,
        grid=(m // tm,),
        in_specs=[pl.BlockSpec((tm, d), lambda i: (i, 0)), pl.BlockSpec((1, d), lambda i: (0, 0))],
        out_specs=pl.BlockSpec((tm, d), lambda i: (i, 0)),
        out_shape=jax.ShapeDtypeStruct((m, d), F32),
        compiler_params=_cparams(1),
        name="final_rmsnorm",
    )(x, g)


def _sortable_key(score):
    bits = pltpu.bitcast(score, I32)
    return bits ^ ((bits >> 31) & jnp.int32(0x7FFFFFFF))


def _kth_largest_key(count_ge, rows, topk):
    zero = jnp.zeros((rows, 1), I32)
    ans = jnp.where(count_ge(zero) >= topk, zero, jnp.full((rows, 1), INT_MIN, I32))

    def bit_body(j, ans):
        cand = ans | lax.shift_left(jnp.int32(1), 30 - j)
        return jnp.where(count_ge(cand) >= topk, cand, ans)

    return lax.fori_loop(0, 31, bit_body, ans)


def _tie_cutoff(count_tie_lt, need, rows, idx_bits):
    def bit_body(j, cut):
        cand = cut | lax.shift_left(jnp.int32(1), idx_bits - 1 - j)
        return jnp.where(count_tie_lt(cand) < need, cand, cut)

    return lax.fori_loop(0, idx_bits, bit_body, jnp.zeros((rows, 1), I32))


def _prompt_attn_kernel(qi_ref, wi_ref, kib_ref, q_ref, kb_ref, vb_ref, ga_ref, o_ref,
                        key_ref, bias_ref, thr_ref, m_ref, l_ref, acc_ref, *, tq, kc, topk, seq):
    i = pl.program_id(1)
    nch = ((i + 1) * tq + kc - 1) // kc
    lane_tiles = kc // LANES
    row_pos = i * tq + lax.broadcasted_iota(I32, (tq, kc), 0)
    lane = lax.broadcasted_iota(I32, (tq, kc), 1)
    idx_bits = max(1, (seq - 1).bit_length())

    qi = qi_ref[...]
    wi = wi_ref[...] * (IDX_DIM ** -0.5)
    wb = [jnp.broadcast_to(wi[:, h:h + 1], (tq, kc)) for h in range(IDX_HEADS)]

    def score_body(c, carry):
        off = pl.multiple_of(c * kc, kc)
        kic = kib_ref[pl.ds(off, kc), :]
        acc = jnp.zeros((tq, kc), F32)
        for h in range(IDX_HEADS):
            s = _dot_nt(qi[:, h * IDX_DIM:(h + 1) * IDX_DIM], kic)
            acc = acc + jnp.maximum(s, 0.0) * wb[h]
        key = _sortable_key(acc)
        key_ref[:, pl.ds(off, kc)] = jnp.where(off + lane <= row_pos, key, INT_MIN)
        return carry

    lax.fori_loop(0, nch, score_body, 0)

    def lane_fold(ind):
        out = ind[:, :LANES]
        for j in range(1, lane_tiles):
            out = out + ind[:, j * LANES:(j + 1) * LANES]
        return out

    def chunk_count(pred):
        def body(c, acc):
            off = pl.multiple_of(c * kc, kc)
            k = key_ref[:, pl.ds(off, kc)]
            return acc + lane_fold(jnp.where(pred(k, off + lane), 1.0, 0.0))
        acc = lax.fori_loop(0, nch, body, jnp.zeros((tq, LANES), F32))
        return jnp.sum(acc, axis=1, keepdims=True)

    def count_ge(cand):
        cb = jnp.broadcast_to(cand, (tq, kc))
        return chunk_count(lambda k, pos: k >= cb)

    def write_bias(keep):
        def body(c, carry):
            off = pl.multiple_of(c * kc, kc)
            k = key_ref[:, pl.ds(off, kc)]
            bias_ref[:, pl.ds(off, kc)] = jnp.where(keep(k, off + lane), 0.0, NEG_BIG)
            return carry
        lax.fori_loop(0, nch, body, 0)

    thr_ref[...] = jnp.full((tq, LANES), INT_MIN + 1, I32)

    @pl.when((i + 1) * tq > topk)
    def _():
        ans = _kth_largest_key(count_ge, tq, topk)
        thr_ref[...] = jnp.broadcast_to(jnp.maximum(ans, INT_MIN + 1), (tq, LANES))

    thr = thr_ref[...]
    thr_b = jnp.concatenate([thr] * lane_tiles, axis=1)
    write_bias(lambda k, pos: k >= thr_b)

    thr1 = thr[:, :1]
    n_ge = count_ge(thr1)

    @pl.when(jnp.max(n_ge) > topk)
    def _():
        need = topk - count_ge(thr1 + 1)

        def count_tie_lt(cut):
            cb = jnp.broadcast_to(cut, (tq, kc))
            return chunk_count(lambda k, pos: jnp.where(k == thr_b, pos, seq) < cb)

        cut_b = jnp.broadcast_to(_tie_cutoff(count_tie_lt, need, tq, idx_bits), (tq, kc))
        write_bias(lambda k, pos: (k > thr_b) | ((k == thr_b) & (pos <= cut_b)))

    scale = HEAD_DIM ** -0.5
    rows = HEADS_PER_KV * tq
    for g in range(KV_HEADS):
        q4 = jnp.concatenate(
            [q_ref[:, (g * HEADS_PER_KV + hh) * HEAD_DIM:(g * HEADS_PER_KV + hh + 1) * HEAD_DIM]
             for hh in range(HEADS_PER_KV)], axis=0)
        m_ref[...] = jnp.full((rows, LANES), NEG_BIG, F32)
        l_ref[...] = jnp.zeros((rows, LANES), F32)
        acc_ref[...] = jnp.zeros((rows, HEAD_DIM), F32)

        def attn_body(c, carry):
            off = pl.multiple_of(c * kc, kc)
            kch = kb_ref[pl.ds(off, kc), g * HEAD_DIM:(g + 1) * HEAD_DIM]
            vch = vb_ref[pl.ds(off, kc), g * HEAD_DIM:(g + 1) * HEAD_DIM]
            b = bias_ref[:, pl.ds(off, kc)]
            s = _dot_nt(q4, kch) * scale + jnp.concatenate([b] * HEADS_PER_KV, axis=0)
            m_prev = m_ref[...]
            m_next = jnp.maximum(m_prev, jnp.max(s, axis=1, keepdims=True))
            p = jnp.exp(s - jnp.concatenate([m_next] * lane_tiles, axis=1))
            alpha = jnp.exp(m_prev - m_next)
            l_ref[...] = alpha * l_ref[...] + jnp.sum(p, axis=1, keepdims=True)
            acc_ref[...] = acc_ref[...] * alpha + jnp.dot(p.astype(BF16), vch, preferred_element_type=F32)
            m_ref[...] = m_next
            return carry

        lax.fori_loop(0, nch, attn_body, 0)
        o = acc_ref[...] / l_ref[...]
        for hh in range(HEADS_PER_KV):
            sl = slice((g * HEADS_PER_KV + hh) * HEAD_DIM, (g * HEADS_PER_KV + hh + 1) * HEAD_DIM)
            o_ref[:, sl] = (o[hh * tq:(hh + 1) * tq] * _silu(ga_ref[:, sl])).astype(BF16)


def _prompt_attention(qi, wi, kib, q, kb, vb, z_rest, batch, seq, tq=128, kc=256):
    nq = seq // tq
    topk = min(INDEX_TOPK, seq // 4)
    attn_w, kv_w, qi_w = N_HEADS * HEAD_DIM, KV_HEADS * HEAD_DIM, IDX_HEADS * IDX_DIM
    blk = lambda b, i: (b * nq + i, 0)
    per_b = lambda b, i: (b, 0)
    rows = HEADS_PER_KV * tq
    return pl.pallas_call(
        functools.partial(_prompt_attn_kernel, tq=tq, kc=kc, topk=topk, seq=seq),
        grid=(batch, nq),
        in_specs=[pl.BlockSpec((tq, qi_w), blk), pl.BlockSpec((tq, IDX_HEADS), blk),
                  pl.BlockSpec((seq, IDX_DIM), per_b), pl.BlockSpec((tq, attn_w), blk),
                  pl.BlockSpec((seq, kv_w), per_b), pl.BlockSpec((seq, kv_w), per_b),
                  pl.BlockSpec((tq, attn_w), blk)],
        out_specs=pl.BlockSpec((tq, attn_w), blk),
        out_shape=jax.ShapeDtypeStruct((batch * seq, attn_w), BF16),
        scratch_shapes=[pltpu.VMEM((tq, seq), I32), pltpu.VMEM((tq, seq), F32),
                        pltpu.VMEM((tq, LANES), I32), pltpu.VMEM((rows, LANES), F32),
                        pltpu.VMEM((rows, LANES), F32), pltpu.VMEM((rows, HEAD_DIM), F32)],
        compiler_params=_cparams(2),
        name="prompt_attention",
    )(qi, wi, kib, q, kb, vb, z_rest)


def _sample_index_kernel(pt_ref, qs_ref, ws_ref, *rest, pages_per_step, t_new):
    page_refs, o_ref = rest[:pages_per_step], rest[pages_per_step]
    qs = qs_ref[...]
    ws = ws_ref[...] * (IDX_DIM ** -0.5)
    page = page_refs[0].shape[0]
    wsb = jnp.broadcast_to(ws, (t_new * IDX_HEADS, page))
    for j in range(pages_per_step):
        kp = page_refs[j][...].astype(BF16)
        s = jnp.maximum(_dot_nt(qs, kp), 0.0) * wsb
        o_ref[:, j * page:(j + 1) * page] = jnp.sum(s.reshape(t_new, IDX_HEADS, page), axis=1)


def _sample_index(page_table, qs, ws, pool_ik, dec_batch, t_new, pages_per_step=8):
    n_pages = page_table.shape[1]
    page = pool_ik.shape[1]
    steps = n_pages // pages_per_step
    rows = t_new * IDX_HEADS

    def page_map(j):
        return lambda b, s, pt: (pt[b, s * pages_per_step + j], 0, 0)

    grid_spec = pltpu.PrefetchScalarGridSpec(
        num_scalar_prefetch=1,
        grid=(dec_batch, steps),
        in_specs=[pl.BlockSpec((rows, IDX_DIM), lambda b, s, pt: (b, 0)),
                  pl.BlockSpec((rows, 1), lambda b, s, pt: (b, 0))]
                 + [pl.BlockSpec((None, page, IDX_DIM), page_map(j)) for j in range(pages_per_step)],
        out_specs=pl.BlockSpec((t_new, pages_per_step * page), lambda b, s, pt: (b, s)),
    )
    return pl.pallas_call(
        functools.partial(_sample_index_kernel, pages_per_step=pages_per_step, t_new=t_new),
        grid_spec=grid_spec,
        out_shape=jax.ShapeDtypeStruct((dec_batch * t_new, n_pages * page), F32),
        compiler_params=_cparams(2),
        name="sample_index",
    )(page_table, qs, ws, *([pool_ik] * pages_per_step))


def _sample_select_kernel(sc_ref, qs_ref, ws_ref, kn_ref, bias_ref, bias_new_ref, key_ref, keyn_ref,
                          *, t_new, past, topk):
    qs = qs_ref[...]
    ws = ws_ref[...] * (IDX_DIM ** -0.5)
    s = jnp.maximum(_dot_nt(qs, kn_ref[...]), 0.0) * jnp.broadcast_to(ws, (t_new * IDX_HEADS, LANES))
    s_new = jnp.sum(s.reshape(t_new, IDX_HEADS, LANES), axis=1) + 0.0
    lane_n = lax.broadcasted_iota(I32, (t_new, LANES), 1)
    row_n = lax.broadcasted_iota(I32, (t_new, LANES), 0)
    keyn_ref[...] = jnp.where(lane_n <= row_n, _sortable_key(s_new), INT_MIN)
    key_ref[...] = _sortable_key(sc_ref[...] + 0.0)
    pos = lax.broadcasted_iota(I32, (t_new, past), 1)
    idx_bits = (past + LANES - 1).bit_length()

    def count(pred):
        a = jnp.sum(jnp.where(pred(key_ref[...], pos), 1.0, 0.0), axis=1, keepdims=True)
        b = jnp.sum(jnp.where(pred(keyn_ref[...], past + lane_n), 1.0, 0.0), axis=1, keepdims=True)
        return a + b

    count_ge = lambda cand: count(lambda k, p: k >= cand)
    thr = jnp.maximum(_kth_largest_key(count_ge, t_new, topk), INT_MIN + 1)
    bias_ref[...] = jnp.where(key_ref[...] >= thr, 0.0, NEG_BIG)
    bias_new_ref[...] = jnp.where(keyn_ref[...] >= thr, 0.0, NEG_BIG)

    @pl.when(jnp.max(count_ge(thr)) > topk)
    def _():
        need = topk - count_ge(thr + 1)
        big = jnp.int32(2 ** idx_bits)
        count_tie_lt = lambda cut: count(lambda k, p: jnp.where(k == thr, p, big) < cut)
        cut = _tie_cutoff(count_tie_lt, need, t_new, idx_bits)
        keep = lambda k, p: (k > thr) | ((k == thr) & (p <= cut))
        bias_ref[...] = jnp.where(keep(key_ref[...], pos), 0.0, NEG_BIG)
        bias_new_ref[...] = jnp.where(keep(keyn_ref[...], past + lane_n), 0.0, NEG_BIG)


def _sample_select(sc, qs, ws, knew_pad, dec_batch, t_new, topk):
    past = sc.shape[1]
    rows = t_new * IDX_HEADS
    return pl.pallas_call(
        functools.partial(_sample_select_kernel, t_new=t_new, past=past, topk=topk),
        grid=(dec_batch,),
        in_specs=[pl.BlockSpec((t_new, past), lambda b: (b, 0)),
                  pl.BlockSpec((rows, IDX_DIM), lambda b: (b, 0)),
                  pl.BlockSpec((rows, 1), lambda b: (b, 0)),
                  pl.BlockSpec((None, LANES, IDX_DIM), lambda b: (b, 0, 0))],
        out_specs=[pl.BlockSpec((t_new, past), lambda b: (b, 0)),
                   pl.BlockSpec((t_new, LANES), lambda b: (b, 0))],
        out_shape=[jax.ShapeDtypeStruct((dec_batch * t_new, past), F32),
                   jax.ShapeDtypeStruct((dec_batch * t_new, LANES), F32)],
        scratch_shapes=[pltpu.VMEM((t_new, past), I32), pltpu.VMEM((t_new, LANES), I32)],
        compiler_params=_cparams(1),
        name="sample_select",
    )(sc, qs, ws, knew_pad)


def _sample_attn_kernel(pt_ref, q_ref, bias_ref, biasn_ref, kn_ref, vn_ref, ga_ref, *rest,
                        pages_per_step, t_new):
    k_refs = rest[:pages_per_step]
    v_refs = rest[pages_per_step:2 * pages_per_step]
    o_ref, m_ref, l_ref, acc_ref = rest[2 * pages_per_step:]
    step = pl.program_id(1)
    rows = HEADS_PER_KV * t_new
    page = k_refs[0].shape[0]
    scale = HEAD_DIM ** -0.5

    @pl.when(step == 0)
    def _():
        m_ref[...] = jnp.full(m_ref.shape, NEG_BIG, F32)
        l_ref[...] = jnp.zeros(l_ref.shape, F32)
        acc_ref[...] = jnp.zeros(acc_ref.shape, F32)

    def q_group(g):
        return jnp.concatenate(
            [q_ref[:, (g * HEADS_PER_KV + hh) * HEAD_DIM:(g * HEADS_PER_KV + hh + 1) * HEAD_DIM]
             for hh in range(HEADS_PER_KV)], axis=0).astype(BF16)

    def update(g, s, v_of):
        m_prev = m_ref[g]
        m_next = jnp.maximum(m_prev, jnp.max(s, axis=1, keepdims=True))
        n = s.shape[1] // LANES
        p = jnp.exp(s - jnp.concatenate([m_next] * n, axis=1))
        alpha = jnp.exp(m_prev - m_next)
        l_ref[g] = alpha * l_ref[g] + jnp.sum(p, axis=1, keepdims=True)
        pv = jnp.dot(p[:, :page].astype(BF16), v_of(0), preferred_element_type=F32)
        for j in range(1, s.shape[1] // page):
            pv = pv + jnp.dot(p[:, j * page:(j + 1) * page].astype(BF16), v_of(j), preferred_element_type=F32)
        acc_ref[g] = acc_ref[g] * alpha + pv
        m_ref[g] = m_next

    bias = bias_ref[...]
    bias4 = jnp.concatenate([bias] * HEADS_PER_KV, axis=0)
    for g in range(KV_HEADS):
        hs = slice(g * HEAD_DIM, (g + 1) * HEAD_DIM)
        qg = q_group(g)
        s = jnp.concatenate([_dot_nt(qg, k_refs[j][:, hs].astype(BF16)) for j in range(pages_per_step)], axis=1)
        update(g, s * scale + bias4, lambda j: v_refs[j][:, hs].astype(BF16))

    @pl.when(step == pl.num_programs(1) - 1)
    def _():
        biasn4 = jnp.concatenate([biasn_ref[...]] * HEADS_PER_KV, axis=0)
        for g in range(KV_HEADS):
            hs = slice(g * HEAD_DIM, (g + 1) * HEAD_DIM)
            s = _dot_nt(q_group(g), kn_ref[:, hs]) * scale + biasn4
            update(g, s, lambda j: vn_ref[:, hs])
            o = acc_ref[g] / l_ref[g]
            for hh in range(HEADS_PER_KV):
                sl = slice((g * HEADS_PER_KV + hh) * HEAD_DIM, (g * HEADS_PER_KV + hh + 1) * HEAD_DIM)
                o_ref[:, sl] = o[hh * t_new:(hh + 1) * t_new] * _silu(ga_ref[:, sl])


def _sample_attention(page_table, q, bias, bias_new, knew_pad, vnew_pad, z_rest, pool_k, pool_v,
                      dec_batch, t_new, pages_per_step=8):
    n_pages = page_table.shape[1]
    page = pool_k.shape[1]
    steps = n_pages // pages_per_step
    attn_w, kv_w = N_HEADS * HEAD_DIM, KV_HEADS * HEAD_DIM
    rows = HEADS_PER_KV * t_new
    per_b = lambda b, s, pt: (b, 0)

    def page_map(j):
        return lambda b, s, pt: (pt[b, s * pages_per_step + j], 0, 0)

    page_specs = [pl.BlockSpec((None, page, kv_w), page_map(j)) for j in range(pages_per_step)]
    grid_spec = pltpu.PrefetchScalarGridSpec(
        num_scalar_prefetch=1,
        grid=(dec_batch, steps),
        in_specs=[pl.BlockSpec((t_new, attn_w), per_b),
                  pl.BlockSpec((t_new, pages_per_step * page), lambda b, s, pt: (b, s)),
                  pl.BlockSpec((t_new, LANES), per_b),
                  pl.BlockSpec((None, LANES, kv_w), lambda b, s, pt: (b, 0, 0)),
                  pl.BlockSpec((None, LANES, kv_w), lambda b, s, pt: (b, 0, 0)),
                  pl.BlockSpec((t_new, attn_w), per_b)] + page_specs + page_specs,
        out_specs=pl.BlockSpec((t_new, attn_w), per_b),
        scratch_shapes=[pltpu.VMEM((KV_HEADS, rows, LANES), F32), pltpu.VMEM((KV_HEADS, rows, LANES), F32),
                        pltpu.VMEM((KV_HEADS, rows, HEAD_DIM), F32)],
    )
    return pl.pallas_call(
        functools.partial(_sample_attn_kernel, pages_per_step=pages_per_step, t_new=t_new),
        grid_spec=grid_spec,
        out_shape=jax.ShapeDtypeStruct((dec_batch * t_new, attn_w), F32),
        compiler_params=_cparams(2),
        name="sample_attention",
    )(page_table, q, bias, bias_new, knew_pad, vnew_pad, z_rest,
      *([pool_k] * pages_per_step), *([pool_v] * pages_per_step))


def _lru_kernel(xr_ref, gr_ref, cp_ref, hp_ref, cw_ref, cb_ref, wr_ref, br_ref, wg_ref, bg_ref, lam_ref,
                o_ref, hl_ref, cn_ref, ext_ref, hc_ref, a_ref, b_ref, *, tt):
    t = pl.program_id(1)
    tail = CONV_W - 1
    base = SUBLANES
    width = xr_ref.shape[1]
    bw = width // LRU_BLOCKS

    @pl.when(t == 0)
    def _():
        ext_ref[base - tail:base, :] = cp_ref[...]
        hc_ref[...] = hp_ref[...]

    x = xr_ref[...]
    ext_ref[base:base + tt, :] = x
    cw = cw_ref[...]
    xc = cb_ref[...] + x * cw[tail:tail + 1, :]
    for j in range(tail):
        xc = xc + ext_ref[base - tail + j: base - tail + j + tt, :] * cw[j:j + 1, :]
    new_tail = x[tt - tail:, :]
    ext_ref[base - tail:base, :] = new_tail

    xcb = xc.astype(BF16)
    r_lin = jnp.concatenate([jnp.dot(xcb[:, n * bw:(n + 1) * bw], wr_ref[n], preferred_element_type=F32)
                             for n in range(LRU_BLOCKS)], axis=1)
    g_lin = jnp.concatenate([jnp.dot(xcb[:, n * bw:(n + 1) * bw], wg_ref[n], preferred_element_type=F32)
                             for n in range(LRU_BLOCKS)], axis=1)
    r = _sigmoid(r_lin + br_ref[...])
    gi = _sigmoid(g_lin + bg_ref[...])
    nl = -lam_ref[...]
    softplus = jnp.maximum(nl, 0.0) + jnp.log1p(jnp.exp(-jnp.abs(nl)))
    log_a = (-LRU_C) * r * softplus
    a = jnp.exp(log_a)
    b = jnp.sqrt(jnp.tanh(-log_a) * (a * a + 1.0)) * (gi * xc)

    row = lax.broadcasted_iota(I32, (tt, width), 0) & (SUBLANES - 1)
    d = 1
    while d < SUBLANES:
        keep = row >= d
        a_s = jnp.where(keep, pltpu.roll(a, d, 0), 1.0)
        b_s = jnp.where(keep, pltpu.roll(b, d, 0), 0.0)
        b = a * b_s + b
        a = a * a_s
        d *= 2
    a_ref[...] = a
    b_ref[...] = b

    def group_body(gidx, h):
        off = pl.multiple_of(gidx * SUBLANES, SUBLANES)
        h8 = a_ref[pl.ds(off, SUBLANES), :] * h + b_ref[pl.ds(off, SUBLANES), :]
        b_ref[pl.ds(off, SUBLANES), :] = h8
        return h8[SUBLANES - 1:SUBLANES, :]

    h_last = lax.fori_loop(0, tt // SUBLANES, group_body, hc_ref[...])
    hc_ref[...] = h_last
    o_ref[...] = (b_ref[...] * _silu(gr_ref[...])).astype(o_ref.dtype)

    @pl.when(t == pl.num_programs(1) - 1)
    def _():
        hl_ref[...] = h_last
        cn_ref[...] = new_tail


def _lru(z_rest, conv_prev, h_prev, cw, cb, wr, br, wg, bg, lam, batch, t_len, tt, out_dtype):
    width = cw.shape[1]
    nt = t_len // tt
    tail = CONV_W - 1
    bw = width // LRU_BLOCKS
    vec = lambda b, t: (0, 0)
    w3 = lambda b, t: (0, 0, 0)
    per_b = lambda b, t: (b, 0, 0)
    return pl.pallas_call(
        functools.partial(_lru_kernel, tt=tt),
        grid=(batch, nt),
        in_specs=[pl.BlockSpec((tt, width), lambda b, t: (b * nt + t, 1)),
                  pl.BlockSpec((tt, width), lambda b, t: (b * nt + t, 2)),
                  pl.BlockSpec((None, tail, width), per_b), pl.BlockSpec((None, 1, width), per_b),
                  pl.BlockSpec((CONV_W, width), vec), pl.BlockSpec((1, width), vec),
                  pl.BlockSpec((LRU_BLOCKS, bw, bw), w3), pl.BlockSpec((1, width), vec),
                  pl.BlockSpec((LRU_BLOCKS, bw, bw), w3), pl.BlockSpec((1, width), vec),
                  pl.BlockSpec((1, width), vec)],
        out_specs=[pl.BlockSpec((tt, width), lambda b, t: (b * nt + t, 0)),
                   pl.BlockSpec((None, 1, width), per_b), pl.BlockSpec((None, tail, width), per_b)],
        out_shape=[jax.ShapeDtypeStruct((batch * t_len, width), out_dtype),
                   jax.ShapeDtypeStruct((batch, 1, width), F32),
                   jax.ShapeDtypeStruct((batch, tail, width), F32)],
        scratch_shapes=[pltpu.VMEM((SUBLANES + tt, width), F32), pltpu.VMEM((1, width), F32),
                        pltpu.VMEM((tt, width), F32), pltpu.VMEM((tt, width), F32)],
        compiler_params=_cparams(2),
        name="rglru",
    )(z_rest, z_rest, conv_prev, h_prev, cw, cb, wr, br, wg, bg, lam)


def _mem_attn_kernel(qm_ref, gm_ref, mk_ref, mv_ref, o_ref):
    width = qm_ref.shape[1]
    hd = width // MEM_HEADS
    scale = hd ** -0.5
    for h in range(MEM_HEADS):
        sl = slice(h * hd, (h + 1) * hd)
        s = _dot_nt(qm_ref[:, sl].astype(BF16), mk_ref[:, sl].astype(BF16)) * scale
        p = jnp.exp(s - jnp.max(s, axis=1, keepdims=True))
        l = jnp.sum(p, axis=1, keepdims=True)
        o = jnp.dot(p.astype(BF16), mv_ref[:, sl].astype(BF16), preferred_element_type=F32) / l
        o_ref[:, sl] = (o * _silu(gm_ref[:, sl])).astype(o_ref.dtype)


def _mem_attention(z_rest, mk, mv, batch, t_len, tm, out_dtype):
    width = mk.shape[2]
    mem = mk.shape[1]
    nt = t_len // tm
    return pl.pallas_call(
        _mem_attn_kernel,
        grid=(batch, nt),
        in_specs=[pl.BlockSpec((tm, width), lambda b, t: (b * nt + t, 3)),
                  pl.BlockSpec((tm, width), lambda b, t: (b * nt + t, 4)),
                  pl.BlockSpec((None, mem, width), lambda b, t: (b, 0, 0)),
                  pl.BlockSpec((None, mem, width), lambda b, t: (b, 0, 0))],
        out_specs=pl.BlockSpec((tm, width), lambda b, t: (b * nt + t, 0)),
        out_shape=jax.ShapeDtypeStruct((batch * t_len, width), out_dtype),
        compiler_params=_cparams(2),
        name="mem_attention",
    )(z_rest, z_rest, mk, mv)


def _merge_kernel(ba_ref, bl_ref, bm_ref, ga_ref, gl_ref, gm_ref, wa_ref, wl_ref, wm_ref, o_ref):
    out = _sigmoid(ga_ref[...]) * jnp.dot(ba_ref[...].astype(BF16), wa_ref[...], preferred_element_type=F32)
    out = out + _sigmoid(gl_ref[...]) * jnp.dot(bl_ref[...].astype(BF16), wl_ref[...], preferred_element_type=F32)
    out = out + _sigmoid(gm_ref[...]) * jnp.dot(bm_ref[...].astype(BF16), wm_ref[...], preferred_element_type=F32)
    o_ref[...] = out.astype(BF16)


def _merge(branches, z_rest, w_branch, tm, tn):
    m, bw = branches[0].shape
    d = w_branch.shape[2]
    gate_col0 = 5 * bw // tn
    br_spec = pl.BlockSpec((tm, bw), lambda j, i: (i, 0))

    def gate_spec(b):
        return pl.BlockSpec((tm, tn), lambda j, i: (i, gate_col0 + b * (d // tn) + j))

    def w_spec(b):
        return pl.BlockSpec((None, bw, tn), lambda j, i: (b, 0, j))

    return pl.pallas_call(
        _merge_kernel,
        grid=(d // tn, m // tm),
        in_specs=[br_spec] * N_BRANCH + [gate_spec(b) for b in range(N_BRANCH)]
                 + [w_spec(b) for b in range(N_BRANCH)],
        out_specs=pl.BlockSpec((tm, tn), lambda j, i: (i, j)),
        out_shape=jax.ShapeDtypeStruct((m, d), BF16),
        compiler_params=_cparams(2),
        name="merge",
    )(*branches, z_rest, z_rest, z_rest, w_branch, w_branch, w_branch)


def _rope_tables(pos, reps):
    def tab(dim):
        half = dim // 2
        freq = ROPE_THETA ** (-jnp.arange(half, dtype=F32) / half)
        ang = pos.astype(F32)[:, None] * freq[None, :]
        cos, sin = jnp.cos(ang), jnp.sin(ang)
        n = LANES // dim
        c = jnp.tile(jnp.concatenate([cos, cos], axis=1), (reps, n))
        s = jnp.tile(jnp.concatenate([-sin, sin], axis=1), (reps, n))
        return c, s
    cq, sq = tab(HEAD_DIM)
    ci, si = tab(IDX_DIM)
    return cq, sq, ci, si


def _pick_tile(m, pref):
    t = min(m, pref)
    while m % t:
        t //= 2
    return t


def kernel(x_prompt, x_sample, mem_prompt, cache_k, cache_v, cache_idx_k, cache_mem_k, cache_mem_v, state_conv, state_h, page_table, norm_g, w_in, conv_w, conv_b, w_rgate, b_rgate, w_igate, b_igate, lru_lambda, mem_norm_g, w_mem_kv, w_branch, w_out, final_norm_g):
    bp, seq, d = x_prompt.shape
    bs, t_new, _ = x_sample.shape
    depth = w_in.shape[0]
    n_pool, page = cache_k.shape[1], cache_k.shape[2]
    past = page_table.shape[1] * page
    mem_tokens = mem_prompt.shape[1]
    attn_w, kv_w, qi_w = N_HEADS * HEAD_DIM, KV_HEADS * HEAD_DIM, IDX_HEADS * IDX_DIM
    lru_w = conv_w.shape[2]
    mem_w = w_mem_kv.shape[2] // 2
    mem_hd = mem_w // MEM_HEADS
    n_attn = attn_w + 2 * kv_w + qi_w
    rest0 = n_attn + IDX_DIM + IDX_HEADS
    topk_s = min(INDEX_TOPK, (past + t_new) // 4)

    tabs_p = _rope_tables(jnp.arange(seq), bp)
    tabs_s = _rope_tables(past + jnp.arange(t_new), bs)

    mp, ms = bp * seq, bs * t_new
    xp = x_prompt.reshape(mp, d)
    xs = x_sample.reshape(ms, d)
    mem2d = mem_prompt.reshape(bp * mem_tokens, d)
    conv0 = jnp.zeros((bp, CONV_W - 1, lru_w), F32)
    h0 = jnp.zeros((bp, 1, lru_w), F32)
    pool_k = cache_k.reshape(depth, n_pool, page, kv_w)
    pool_v = cache_v.reshape(depth, n_pool, page, kv_w)

    tm_p = _pick_tile(mp, 256)
    tm_mm = _pick_tile(mp, 1024)

    st_p, st_s, mk_list, mv_list = [], [], [], []
    for l in range(depth):
        w_l = w_in[l]
        pad = lambda a: jnp.pad(a, ((0, 0), (0, LANES - a.shape[1])))
        w_a = jnp.concatenate([w_l[:, :n_attn], pad(w_l[:, n_attn:n_attn + IDX_DIM]),
                               pad(w_l[:, n_attn + IDX_DIM:rest0])], axis=1).astype(BF16)
        w_rest = w_l[:, rest0:].astype(BF16)
        w_br = w_branch[l].astype(BF16)
        w_o = w_out[l].astype(BF16)
        g_l = norm_g[l].reshape(1, d)
        lru_args = (conv_w[l], conv_b[l].reshape(1, lru_w), w_rgate[l].astype(BF16), b_rgate[l].reshape(1, lru_w),
                    w_igate[l].astype(BF16), b_igate[l].reshape(1, lru_w), lru_lambda[l].reshape(1, lru_w))

        m_kv = _norm_matmul(mem2d, mem_norm_g[l].reshape(1, d), w_mem_kv[l].astype(BF16),
                            _pick_tile(bp * mem_tokens, 512), 1024)
        mk = m_kv[:, :mem_w].reshape(bp, mem_tokens, mem_w)
        mv = m_kv[:, mem_w:].reshape(bp, mem_tokens, mem_w)
        mk_list.append(mk.reshape(bp, mem_tokens, MEM_HEADS, mem_hd))
        mv_list.append(mv.reshape(bp, mem_tokens, MEM_HEADS, mem_hd))

        xn, q, k, v, qi, ki, wi, kb, vb, kib = _proj_attn(xp, g_l, w_a, tabs_p, tm_p)
        z_rest = _matmul(xn, w_rest, tm_mm, 1024)
        br_att = _prompt_attention(qi, wi, kib, q, kb, vb, z_rest, bp, seq)
        br_lru, h_last, conv_new = _lru(z_rest, conv0, h0, *lru_args, bp, seq, _pick_tile(seq, 256), BF16)
        br_mem = _mem_attention(z_rest, mk, mv, bp, seq, _pick_tile(seq, 512), BF16)
        merged = _merge([br_att, br_lru, br_mem], z_rest, w_br, _pick_tile(mp, 512), 512)
        xp = _matmul(merged, w_o, _pick_tile(mp, 512), 1024, residual=xp)
        st_p.append((k.reshape(bp, seq, KV_HEADS, HEAD_DIM), v.reshape(bp, seq, KV_HEADS, HEAD_DIM),
                     ki.reshape(bp, seq, IDX_DIM), conv_new, h_last.reshape(bp, lru_w)))

        xn, q, k, v, qi, ki, wi, kb, vb, kib = _proj_attn(xs, g_l, w_a, tabs_s, ms)
        z_rest = _matmul(xn, w_rest, ms, 1024)
        qs = qi.reshape(ms * IDX_HEADS, IDX_DIM)
        ws = wi.reshape(ms * IDX_HEADS, 1)
        pad_rows = lambda a: jnp.pad(a.reshape(bs, t_new, a.shape[1]), ((0, 0), (0, LANES - t_new), (0, 0)))
        sc = _sample_index(page_table, qs, ws, cache_idx_k[l], bs, t_new)
        bias, bias_new = _sample_select(sc, qs, ws, pad_rows(kib), bs, t_new, topk_s)
        br_att = _sample_attention(page_table, q.astype(F32), bias, bias_new, pad_rows(kb), pad_rows(vb), z_rest,
                                   pool_k[l], pool_v[l], bs, t_new)
        br_lru, h_last, conv_new = _lru(z_rest, state_conv[l], state_h[l].reshape(bs, 1, lru_w), *lru_args,
                                        bs, t_new, t_new, F32)
        br_mem = _mem_attention(z_rest, cache_mem_k[l].reshape(bs, mem_tokens, mem_w),
                                cache_mem_v[l].reshape(bs, mem_tokens, mem_w), bs, t_new, t_new, F32)
        merged = _merge([br_att, br_lru, br_mem], z_rest, w_br, ms, 512)
        xs = _matmul(merged, w_o, ms, 1024, residual=xs)
        st_s.append((k.reshape(bs, t_new, KV_HEADS, HEAD_DIM), v.reshape(bs, t_new, KV_HEADS, HEAD_DIM),
                     ki.reshape(bs, t_new, IDX_DIM), conv_new, h_last.reshape(bs, lru_w)))

    g_f = final_norm_g.reshape(1, d)
    y_prompt = _rmsnorm(xp, g_f, _pick_tile(mp, 512)).reshape(bp, seq, d)
    y_sample = _rmsnorm(xs, g_f, ms).reshape(bs, t_new, d)
    k_p, v_p, ik_p, conv_p, h_p = [jnp.stack(t) for t in zip(*st_p)]
    k_s, v_s, ik_s, conv_s, h_s = [jnp.stack(t) for t in zip(*st_s)]
    return (y_prompt, y_sample, k_p, v_p, ik_p, conv_p, h_p, jnp.stack(mk_list), jnp.stack(mv_list),
            k_s, v_s, ik_s, conv_s, h_s)
```

```python
import functools

import jax
import jax.numpy as jnp
from jax import lax
from jax.experimental import pallas as pl
from jax.experimental.pallas import tpu as pltpu

F32 = jnp.float32
BF16 = jnp.bfloat16
I32 = jnp.int32

N_HEADS = 8
HEAD_DIM = 128
KV_HEADS = 2
HEADS_PER_KV = N_HEADS // KV_HEADS
IDX_HEADS = 8
IDX_DIM = 64
INDEX_TOPK = 256
LRU_BLOCKS = 8
CONV_W = 4
LRU_C = 8.0
MEM_HEADS = 4
N_BRANCH = 3
ROPE_THETA = 10000.0
NORM_EPS = 1e-6

ATTN_W = N_HEADS * HEAD_DIM
KV_W = KV_HEADS * HEAD_DIM
QI_W = IDX_HEADS * IDX_DIM

LANES = 128
SUBLANES = 8
VMEM_LIMIT = 52 * 1024 * 1024
CAST_ROWS = 256
COUNT_ROWS = 8 * SUBLANES
LOG2E = 1.4426950408889634

NEG_BIG = -1e30
INT_MIN = -(2 ** 31)


def _cparams(n_axes):
    return pltpu.CompilerParams(dimension_semantics=("arbitrary",) * n_axes,
                                vmem_limit_bytes=VMEM_LIMIT)


def _sigmoid(x):
    return 1.0 / (1.0 + jnp.exp(-x))


def _silu(x):
    return x * _sigmoid(x)


def _dot(a, b):
    return jnp.dot(a, b, preferred_element_type=F32)


def _dot_nt(a, b):
    return lax.dot_general(a, b, (((1,), (1,)), ((), ())), preferred_element_type=F32)


def _rms(x, g):
    ms = jnp.mean(x * x, axis=-1, keepdims=True)
    return (x * lax.rsqrt(ms + NORM_EPS)) * g


def _cast_weight(dst_ref, src_ref):
    k = dst_ref.shape[0]
    for r in range(0, k, CAST_ROWS):
        rows = slice(r, min(r + CAST_ROWS, k))
        dst_ref[rows, :] = src_ref[rows, :].astype(BF16)


def _proj_attn_kernel(x_ref, g_ref, w_ref, cq_ref, sq_ref, ci_ref, si_ref,
                      xn_ref, k_ref, v_ref, ki_ref, kb_ref, kib_ref, q_ref, qi_ref, wi_ref, vb_ref,
                      wbf_ref, *, layer, transposed):
    @pl.when(pl.program_id(0) == 0)
    def _():
        _cast_weight(wbf_ref, w_ref)

    xnb = _rms(x_ref[...], g_ref[layer:layer + 1, :]).astype(BF16)
    xn_ref[...] = xnb
    z = _dot_nt(xnb, wbf_ref[...])

    cq, sq, ci, si = cq_ref[...], sq_ref[...], ci_ref[...], si_ref[...]
    lane = lax.broadcasted_iota(I32, cq.shape, 1)
    first_half = (lane & (IDX_DIM // 2)) == 0

    def rope128(t):
        return t * cq + pltpu.roll(t, HEAD_DIM // 2, 1) * sq

    def rope64(t):
        rot = jnp.where(first_half, pltpu.roll(t, LANES - IDX_DIM // 2, 1),
                        pltpu.roll(t, IDX_DIM // 2, 1))
        return t * ci + rot * si

    def tile(c):
        return z[:, c * LANES:(c + 1) * LANES]

    for h in range(N_HEADS):
        qh = rope128(tile(h))
        if transposed:
            q_ref[h * HEAD_DIM:(h + 1) * HEAD_DIM, :] = qh.T.astype(BF16)
        else:
            q_ref[:, h * HEAD_DIM:(h + 1) * HEAD_DIM] = qh
    c0 = ATTN_W // LANES
    for h in range(KV_HEADS):
        sl = slice(h * HEAD_DIM, (h + 1) * HEAD_DIM)
        kr = rope128(tile(c0 + h))
        k_ref[:, sl] = kr
        kb_ref[:, sl] = kr.astype(BF16)
    c0 += KV_W // LANES
    for h in range(KV_HEADS):
        sl = slice(h * HEAD_DIM, (h + 1) * HEAD_DIM)
        vh = tile(c0 + h)
        v_ref[:, sl] = vh
        if transposed:
            vb_ref[sl, :] = vh.T.astype(BF16)
        else:
            vb_ref[:, sl] = vh.astype(BF16)
    c0 += KV_W // LANES
    for c in range(QI_W // LANES):
        qc = rope64(tile(c0 + c))
        if transposed:
            qi_ref[c * LANES:(c + 1) * LANES, :] = qc.T.astype(BF16)
        else:
            qi_ref[:, c * LANES:(c + 1) * LANES] = qc.astype(BF16)
    c0 += QI_W // LANES
    last = tile(c0)
    kir = rope64(last)[:, :IDX_DIM]
    ki_ref[...] = kir
    kib_ref[...] = kir.astype(BF16)
    if transposed:
        wi_ref[...] = last.T[IDX_DIM:IDX_DIM + IDX_HEADS, :] * (IDX_HEADS ** -0.5)
    else:
        wi_ref[...] = last[:, IDX_DIM:IDX_DIM + IDX_HEADS] * (IDX_HEADS ** -0.5)


def _proj_attn(x, norm_g, w_in_t, tabs, layer, tm, transposed):
    m, d = x.shape
    na = ATTN_W + 2 * KV_W + QI_W + LANES
    row = lambda i: (i, 0)
    col = lambda i: (0, i)
    const = lambda i: (0, 0)

    def tok(width, dtype):
        return ((m, width), dtype, pl.BlockSpec((tm, width), row))

    def feat(width, dtype):
        return ((width, m), dtype, pl.BlockSpec((width, tm), col))

    outs = [tok(d, BF16), tok(KV_W, F32), tok(KV_W, F32), tok(IDX_DIM, F32), tok(KV_W, BF16), tok(IDX_DIM, BF16)]
    if transposed:
        outs += [feat(ATTN_W, BF16), feat(QI_W, BF16), feat(IDX_HEADS, F32), feat(KV_W, BF16)]
    else:
        outs += [tok(ATTN_W, F32), tok(QI_W, BF16), tok(IDX_HEADS, F32), tok(KV_W, BF16)]
    return pl.pallas_call(
        functools.partial(_proj_attn_kernel, layer=layer, transposed=transposed),
        grid=(m // tm,),
        in_specs=[pl.BlockSpec((tm, d), row), pl.BlockSpec(norm_g.shape, const),
                  pl.BlockSpec((None, na, d), lambda i: (layer, 0, 0), pipeline_mode=pl.Buffered(1))]
                 + [pl.BlockSpec((tm, LANES), row)] * 4,
        out_specs=[o[2] for o in outs],
        out_shape=[jax.ShapeDtypeStruct(o[0], o[1]) for o in outs],
        scratch_shapes=[pltpu.VMEM((na, d), BF16)],
        compiler_params=_cparams(1),
        name="proj_attn",
    )(x, norm_g, w_in_t, *tabs)


def _mm_kernel(*refs, w_transposed, residual):
    x_ref, w_ref = refs[0], refs[1]
    r_ref = refs[2] if residual else None
    o_ref, wbf_ref = refs[-2], refs[-1]

    @pl.when(pl.program_id(1) == 0)
    def _():
        _cast_weight(wbf_ref, w_ref.at[0] if w_transposed else w_ref)

    out = _dot_nt(x_ref[...], wbf_ref[...]) if w_transposed else _dot(x_ref[...], wbf_ref[...])
    if residual:
        out = r_ref[...] + out
    o_ref[...] = out


def _matmul(x, w, layer, col0, n, tm, tn, w_transposed, residual=None):
    m, k = x.shape
    assert n % tn == 0 and m % tm == 0
    if w_transposed:
        assert col0 % SUBLANES == 0
        w_spec = pl.BlockSpec((pl.Element(1), pl.Element(tn), pl.Element(k)),
                              lambda j, i: (layer, (col0 // SUBLANES + j * (tn // SUBLANES)) * SUBLANES, 0))
        w_scratch = pltpu.VMEM((tn, k), BF16)
    else:
        assert col0 % tn == 0
        w_spec = pl.BlockSpec((None, k, tn), lambda j, i: (layer, 0, col0 // tn + j))
        w_scratch = pltpu.VMEM((k, tn), BF16)
    in_specs = [pl.BlockSpec((tm, k), lambda j, i: (i, 0)), w_spec]
    args = [x, w]
    if residual is not None:
        in_specs.append(pl.BlockSpec((tm, tn), lambda j, i: (i, j)))
        args.append(residual)
    return pl.pallas_call(
        functools.partial(_mm_kernel, w_transposed=w_transposed, residual=residual is not None),
        grid=(n // tn, m // tm),
        in_specs=in_specs,
        out_specs=pl.BlockSpec((tm, tn), lambda j, i: (i, j)),
        out_shape=jax.ShapeDtypeStruct((m, n), F32),
        scratch_shapes=[w_scratch],
        compiler_params=_cparams(2),
        name="matmul_residual" if residual is not None else "matmul",
    )(*args)


def _norm_mm_kernel(x_ref, g_ref, w_ref, o_ref, *, layer):
    xn = _rms(x_ref[...], g_ref[layer:layer + 1, :]).astype(BF16)
    o_ref[...] = _dot(xn, w_ref[...].astype(BF16))


def _norm_matmul(x, g, w, layer, tm, tn):
    m, k = x.shape
    n = w.shape[2]
    return pl.pallas_call(
        functools.partial(_norm_mm_kernel, layer=layer),
        grid=(n // tn, m // tm),
        in_specs=[pl.BlockSpec((tm, k), lambda j, i: (i, 0)), pl.BlockSpec(g.shape, lambda j, i: (0, 0)),
                  pl.BlockSpec((None, k, tn), lambda j, i: (layer, 0, j))],
        out_specs=pl.BlockSpec((tm, tn), lambda j, i: (i, j)),
        out_shape=jax.ShapeDtypeStruct((m, n), F32),
        compiler_params=_cparams(2),
        name="norm_matmul",
    )(x, g, w)


def _rmsnorm_kernel(x_ref, g_ref, o_ref):
    o_ref[...] = _rms(x_ref[...], g_ref[...])


def _rmsnorm(x, g, tm):
    m, d = x.shape
    return pl.pallas_call(
        _rmsnorm_kernel,
        grid=(m // tm,),
        in_specs=[pl.BlockSpec((tm, d), lambda i: (i, 0)), pl.BlockSpec((1, d), lambda i: (0, 0))],
        out_specs=pl.BlockSpec((tm, d), lambda i: (i, 0)),
        out_shape=jax.ShapeDtypeStruct((m, d), F32),
        compiler_params=_cparams(1),
        name="final_rmsnorm",
    )(x, g)


def _sortable_key(score):
    bits = pltpu.bitcast(score, I32)
    return bits ^ ((bits >> 31) & jnp.int32(0x7FFFFFFF))


def _kth_largest_key(count_ge, shape, topk):
    zero = jnp.zeros(shape, I32)
    ans = jnp.where(count_ge(zero) >= topk, zero, jnp.full(shape, INT_MIN, I32))

    def bit_body(j, ans):
        cand = ans | lax.shift_left(jnp.int32(1), 30 - j)
        return jnp.where(count_ge(cand) >= topk, cand, ans)

    return lax.fori_loop(0, 31, bit_body, ans)


def _tie_cutoff(count_tie_lt, need, shape, idx_bits):
    def bit_body(j, cut):
        cand = cut | lax.shift_left(jnp.int32(1), idx_bits - 1 - j)
        return jnp.where(count_tie_lt(cand) < need, cand, cut)

    return lax.fori_loop(0, idx_bits, bit_body, jnp.zeros(shape, I32))


def _prompt_attn_kernel(qi_ref, wi_ref, kib_ref, q_ref, kb_ref, vb_ref, ga_ref, o_ref,
                        key_ref, bias_ref, acc_ref, *, tq, kc, topk, seq):
    i = pl.program_id(1)
    nch = ((i + 1) * tq + kc - 1) // kc
    key_pos = lax.broadcasted_iota(I32, (kc, tq), 0)
    q_pos = i * tq + lax.broadcasted_iota(I32, (kc, tq), 1)
    idx_bits = max(1, (seq - 1).bit_length())
    row1 = (1, tq)

    def chunk(ref, c):
        return ref[pl.ds(pl.multiple_of(c * kc, kc), kc), :]

    wi = wi_ref[...] * (IDX_DIM ** -0.5)

    def score_body(c, carry):
        kic = chunk(kib_ref, c)
        acc = jnp.zeros((kc, tq), F32)
        for h in range(IDX_HEADS):
            s = _dot(kic, qi_ref[h * IDX_DIM:(h + 1) * IDX_DIM, :])
            acc = acc + jnp.maximum(s, 0.0) * wi[h:h + 1, :]
        key = jnp.where(c * kc + key_pos <= q_pos, _sortable_key(acc), INT_MIN)
        key_ref[pl.ds(pl.multiple_of(c * kc, kc), kc), :] = key
        return carry

    lax.fori_loop(0, nch, score_body, 0)

    def chunk_count(pred):
        def body(c, acc):
            ind = jnp.where(pred(chunk(key_ref, c), c * kc + key_pos), 1.0, 0.0)
            return acc + jnp.sum(ind.reshape(kc // COUNT_ROWS, COUNT_ROWS, tq), axis=0)
        acc = lax.fori_loop(0, nch, body, jnp.zeros((COUNT_ROWS, tq), F32))
        return jnp.sum(acc, axis=0, keepdims=True)

    def count_ge(cand):
        return chunk_count(lambda k, pos: k >= cand)

    def write_bias(keep):
        def body(c, carry):
            k = chunk(key_ref, c)
            bias_ref[pl.ds(pl.multiple_of(c * kc, kc), kc), :] = jnp.where(keep(k, c * kc + key_pos), 0.0, NEG_BIG)
            return carry
        lax.fori_loop(0, nch, body, 0)

    need_search = (i + 1) * tq > topk
    thr = lax.cond(need_search,
                   lambda: jnp.maximum(_kth_largest_key(count_ge, row1, topk), INT_MIN + 1),
                   lambda: jnp.full(row1, INT_MIN + 1, I32))
    write_bias(lambda k, pos: k >= thr)

    @pl.when(jnp.max(count_ge(thr)) > topk)
    def _():
        need = topk - count_ge(thr + 1)
        count_tie_lt = lambda cut: chunk_count(lambda k, pos: jnp.where(k == thr, pos, seq) < cut)
        cut = _tie_cutoff(count_tie_lt, need, row1, idx_bits)
        write_bias(lambda k, pos: (k > thr) | ((k == thr) & (pos <= cut)))

    scale = HEAD_DIM ** -0.5 * LOG2E
    qw = HEADS_PER_KV * tq
    q4 = [jnp.concatenate([q_ref[(g * HEADS_PER_KV + hh) * HEAD_DIM:(g * HEADS_PER_KV + hh + 1) * HEAD_DIM, :]
                           for hh in range(HEADS_PER_KV)], axis=1) for g in range(KV_HEADS)]
    acc_ref[...] = jnp.zeros(acc_ref.shape, F32)

    def attn_body(c, carry):
        off = pl.multiple_of(c * kc, kc)
        bias = chunk(bias_ref, c)
        bias4 = jnp.concatenate([bias] * HEADS_PER_KV, axis=1)
        out = []
        for g in range(KV_HEADS):
            m_prev, l_prev = carry[g]
            hs = slice(g * HEAD_DIM, (g + 1) * HEAD_DIM)
            s = _dot(kb_ref[pl.ds(off, kc), hs], q4[g]) * scale + bias4
            m_next = jnp.maximum(m_prev, jnp.max(s, axis=0, keepdims=True))
            p = jnp.exp2(s - m_next)
            alpha = jnp.exp2(m_prev - m_next)
            l_next = alpha * l_prev + jnp.sum(p, axis=0, keepdims=True)
            acc_ref[g] = acc_ref[g] * alpha + _dot(vb_ref[hs, pl.ds(off, kc)], p.astype(BF16))
            out.append((m_next, l_next))
        return tuple(out)

    init = tuple((jnp.full((1, qw), NEG_BIG, F32), jnp.zeros((1, qw), F32)) for _ in range(KV_HEADS))
    stats = lax.fori_loop(0, nch, attn_body, init)
    for g in range(KV_HEADS):
        o_t = acc_ref[g] / stats[g][1]
        for hh in range(HEADS_PER_KV):
            sl = slice((g * HEADS_PER_KV + hh) * HEAD_DIM, (g * HEADS_PER_KV + hh + 1) * HEAD_DIM)
            o_ref[:, sl] = (o_t[:, hh * tq:(hh + 1) * tq].T * _silu(ga_ref[:, sl])).astype(BF16)


def _prompt_attention(qi_t, wi_t, kib, q_t, kb, vb_t, z_rest, batch, seq, tq=128):
    nq = seq // tq
    kc = _pick_tile(seq, 512)
    topk = min(INDEX_TOPK, seq // 4)
    blk_t = lambda b, i: (0, b * nq + i)
    blk = lambda b, i: (b * nq + i, 0)
    return pl.pallas_call(
        functools.partial(_prompt_attn_kernel, tq=tq, kc=kc, topk=topk, seq=seq),
        grid=(batch, nq),
        in_specs=[pl.BlockSpec((QI_W, tq), blk_t), pl.BlockSpec((IDX_HEADS, tq), blk_t),
                  pl.BlockSpec((seq, IDX_DIM), lambda b, i: (b, 0)), pl.BlockSpec((ATTN_W, tq), blk_t),
                  pl.BlockSpec((seq, KV_W), lambda b, i: (b, 0)), pl.BlockSpec((KV_W, seq), lambda b, i: (0, b)),
                  pl.BlockSpec((tq, ATTN_W), blk)],
        out_specs=pl.BlockSpec((tq, ATTN_W), blk),
        out_shape=jax.ShapeDtypeStruct((batch * seq, ATTN_W), BF16),
        scratch_shapes=[pltpu.VMEM((seq, tq), I32), pltpu.VMEM((seq, tq), F32),
                        pltpu.VMEM((KV_HEADS, HEAD_DIM, HEADS_PER_KV * tq), F32)],
        compiler_params=_cparams(2),
        name="prompt_attention",
    )(qi_t, wi_t, kib, q_t, kb, vb_t, z_rest)


def _sample_index_kernel(pt_ref, qs_ref, ws_ref, *rest, pages_per_step, t_new):
    page_refs, o_ref = rest[:pages_per_step], rest[pages_per_step]
    qs = qs_ref[...]
    ws = ws_ref[...] * (IDX_DIM ** -0.5)
    page = page_refs[0].shape[1]
    wsb = jnp.broadcast_to(ws, (t_new * IDX_HEADS, page))
    for j in range(pages_per_step):
        kp = page_refs[j][...].astype(BF16)
        s = jnp.maximum(_dot(qs, kp), 0.0) * wsb
        o_ref[:, j * page:(j + 1) * page] = jnp.sum(s.reshape(t_new, IDX_HEADS, page), axis=1)


def _sample_index(page_table, qs, ws, pool_ik_t, layer, dec_batch, t_new, pages_per_step):
    n_pages = page_table.shape[1]
    page = pool_ik_t.shape[3]
    steps = n_pages // pages_per_step
    rows = t_new * IDX_HEADS

    def page_map(j):
        return lambda b, s, pt: (layer, pt[b, s * pages_per_step + j], 0, 0)

    grid_spec = pltpu.PrefetchScalarGridSpec(
        num_scalar_prefetch=1,
        grid=(dec_batch, steps),
        in_specs=[pl.BlockSpec((rows, IDX_DIM), lambda b, s, pt: (b, 0)),
                  pl.BlockSpec((rows, 1), lambda b, s, pt: (b, 0))]
                 + [pl.BlockSpec((None, None, IDX_DIM, page), page_map(j)) for j in range(pages_per_step)],
        out_specs=pl.BlockSpec((t_new, pages_per_step * page), lambda b, s, pt: (b, s)),
    )
    return pl.pallas_call(
        functools.partial(_sample_index_kernel, pages_per_step=pages_per_step, t_new=t_new),
        grid_spec=grid_spec,
        out_shape=jax.ShapeDtypeStruct((dec_batch * t_new, n_pages * page), F32),
        compiler_params=_cparams(2),
        name="sample_index",
    )(page_table, qs, ws, *([pool_ik_t] * pages_per_step))


def _sample_select_kernel(sc_ref, qs_ref, ws_ref, kn_ref, bias_ref, bias_new_ref, key_ref, keyn_ref,
                          *, t_new, past, topk):
    qs = qs_ref[...]
    ws = ws_ref[...] * (IDX_DIM ** -0.5)
    s = jnp.maximum(_dot_nt(qs, kn_ref[...]), 0.0) * jnp.broadcast_to(ws, (t_new * IDX_HEADS, LANES))
    s_new = jnp.sum(s.reshape(t_new, IDX_HEADS, LANES), axis=1) + 0.0
    lane_n = lax.broadcasted_iota(I32, (t_new, LANES), 1)
    row_n = lax.broadcasted_iota(I32, (t_new, LANES), 0)
    keyn_ref[...] = jnp.where(lane_n <= row_n, _sortable_key(s_new), INT_MIN)
    key_ref[...] = _sortable_key(sc_ref[...] + 0.0)
    pos = lax.broadcasted_iota(I32, (t_new, past), 1)
    idx_bits = (past + LANES - 1).bit_length()
    col1 = (t_new, 1)

    def count(pred):
        a = jnp.sum(jnp.where(pred(key_ref[...], pos), 1.0, 0.0), axis=1, keepdims=True)
        b = jnp.sum(jnp.where(pred(keyn_ref[...], past + lane_n), 1.0, 0.0), axis=1, keepdims=True)
        return a + b

    count_ge = lambda cand: count(lambda k, p: k >= cand)
    thr = jnp.maximum(_kth_largest_key(count_ge, col1, topk), INT_MIN + 1)
    bias_ref[...] = jnp.where(key_ref[...] >= thr, 0.0, NEG_BIG)
    bias_new_ref[...] = jnp.where(keyn_ref[...] >= thr, 0.0, NEG_BIG)

    @pl.when(jnp.max(count_ge(thr)) > topk)
    def _():
        need = topk - count_ge(thr + 1)
        big = jnp.int32(2 ** idx_bits)
        count_tie_lt = lambda cut: count(lambda k, p: jnp.where(k == thr, p, big) < cut)
        cut = _tie_cutoff(count_tie_lt, need, col1, idx_bits)
        keep = lambda k, p: (k > thr) | ((k == thr) & (p <= cut))
        bias_ref[...] = jnp.where(keep(key_ref[...], pos), 0.0, NEG_BIG)
        bias_new_ref[...] = jnp.where(keep(keyn_ref[...], past + lane_n), 0.0, NEG_BIG)


def _sample_select(sc, qs, ws, knew_pad, dec_batch, t_new, topk):
    past = sc.shape[1]
    rows = t_new * IDX_HEADS
    return pl.pallas_call(
        functools.partial(_sample_select_kernel, t_new=t_new, past=past, topk=topk),
        grid=(dec_batch,),
        in_specs=[pl.BlockSpec((t_new, past), lambda b: (b, 0)),
                  pl.BlockSpec((rows, IDX_DIM), lambda b: (b, 0)),
                  pl.BlockSpec((rows, 1), lambda b: (b, 0)),
                  pl.BlockSpec((None, LANES, IDX_DIM), lambda b: (b, 0, 0))],
        out_specs=[pl.BlockSpec((t_new, past), lambda b: (b, 0)),
                   pl.BlockSpec((t_new, LANES), lambda b: (b, 0))],
        out_shape=[jax.ShapeDtypeStruct((dec_batch * t_new, past), F32),
                   jax.ShapeDtypeStruct((dec_batch * t_new, LANES), F32)],
        scratch_shapes=[pltpu.VMEM((t_new, past), I32), pltpu.VMEM((t_new, LANES), I32)],
        compiler_params=_cparams(1),
        name="sample_select",
    )(sc, qs, ws, knew_pad)


def _sample_attn_kernel(pt_ref, q_ref, bias_ref, biasn_ref, kn_ref, vn_ref, ga_ref, *rest,
                        pages_per_step, t_new):
    k_refs = rest[:pages_per_step]
    v_refs = rest[pages_per_step:2 * pages_per_step]
    o_ref, m_ref, l_ref, acc_ref = rest[2 * pages_per_step:]
    step = pl.program_id(1)
    page = k_refs[0].shape[0] // KV_HEADS
    scale = HEAD_DIM ** -0.5

    @pl.when(step == 0)
    def _():
        m_ref[...] = jnp.full(m_ref.shape, NEG_BIG, F32)
        l_ref[...] = jnp.zeros(l_ref.shape, F32)
        acc_ref[...] = jnp.zeros(acc_ref.shape, F32)

    def q_group(g):
        return jnp.concatenate(
            [q_ref[:, (g * HEADS_PER_KV + hh) * HEAD_DIM:(g * HEADS_PER_KV + hh + 1) * HEAD_DIM]
             for hh in range(HEADS_PER_KV)], axis=0).astype(BF16)

    def head_rows(ref, g):
        return ref[pl.ds(g, page, stride=KV_HEADS), :].astype(BF16)

    def update(g, s, v_of):
        m_prev = m_ref[g]
        m_next = jnp.maximum(m_prev, jnp.max(s, axis=1, keepdims=True))
        n = s.shape[1] // LANES
        p = jnp.exp(s - jnp.concatenate([m_next] * n, axis=1))
        alpha = jnp.exp(m_prev - m_next)
        l_ref[g] = alpha * l_ref[g] + jnp.sum(p, axis=1, keepdims=True)
        pv = _dot(p[:, :page].astype(BF16), v_of(0))
        for j in range(1, s.shape[1] // page):
            pv = pv + _dot(p[:, j * page:(j + 1) * page].astype(BF16), v_of(j))
        acc_ref[g] = acc_ref[g] * alpha + pv
        m_ref[g] = m_next

    bias4 = jnp.concatenate([bias_ref[...]] * HEADS_PER_KV, axis=0)
    for g in range(KV_HEADS):
        qg = q_group(g)
        s = jnp.concatenate([_dot_nt(qg, head_rows(k_refs[j], g)) for j in range(pages_per_step)], axis=1)
        update(g, s * scale + bias4, lambda j: head_rows(v_refs[j], g))

    @pl.when(step == pl.num_programs(1) - 1)
    def _():
        biasn4 = jnp.concatenate([biasn_ref[...]] * HEADS_PER_KV, axis=0)
        for g in range(KV_HEADS):
            hs = slice(g * HEAD_DIM, (g + 1) * HEAD_DIM)
            s = _dot_nt(q_group(g), kn_ref[:, hs]) * scale + biasn4
            update(g, s, lambda j: vn_ref[:, hs])
            o = acc_ref[g] / l_ref[g]
            for hh in range(HEADS_PER_KV):
                sl = slice((g * HEADS_PER_KV + hh) * HEAD_DIM, (g * HEADS_PER_KV + hh + 1) * HEAD_DIM)
                o_ref[:, sl] = o[hh * t_new:(hh + 1) * t_new] * _silu(ga_ref[:, sl])


def _sample_attention(page_table, q, bias, bias_new, knew_pad, vnew_pad, z_rest, pool_k, pool_v, layer,
                      dec_batch, t_new, pages_per_step):
    n_pages = page_table.shape[1]
    page_rows = pool_k.shape[2]
    page = page_rows // KV_HEADS
    steps = n_pages // pages_per_step
    rows = HEADS_PER_KV * t_new
    per_b = lambda b, s, pt: (b, 0)

    def page_map(j):
        return lambda b, s, pt: (layer, pt[b, s * pages_per_step + j], 0, 0)

    page_specs = [pl.BlockSpec((None, None, page_rows, HEAD_DIM), page_map(j)) for j in range(pages_per_step)]
    grid_spec = pltpu.PrefetchScalarGridSpec(
        num_scalar_prefetch=1,
        grid=(dec_batch, steps),
        in_specs=[pl.BlockSpec((t_new, ATTN_W), per_b),
                  pl.BlockSpec((t_new, pages_per_step * page), lambda b, s, pt: (b, s)),
                  pl.BlockSpec((t_new, LANES), per_b),
                  pl.BlockSpec((None, LANES, KV_W), lambda b, s, pt: (b, 0, 0)),
                  pl.BlockSpec((None, LANES, KV_W), lambda b, s, pt: (b, 0, 0)),
                  pl.BlockSpec((t_new, ATTN_W), per_b)] + page_specs + page_specs,
        out_specs=pl.BlockSpec((t_new, ATTN_W), per_b),
        scratch_shapes=[pltpu.VMEM((KV_HEADS, rows, LANES), F32), pltpu.VMEM((KV_HEADS, rows, LANES), F32),
                        pltpu.VMEM((KV_HEADS, rows, HEAD_DIM), F32)],
    )
    return pl.pallas_call(
        functools.partial(_sample_attn_kernel, pages_per_step=pages_per_step, t_new=t_new),
        grid_spec=grid_spec,
        out_shape=jax.ShapeDtypeStruct((dec_batch * t_new, ATTN_W), F32),
        compiler_params=_cparams(2),
        name="sample_attention",
    )(page_table, q, bias, bias_new, knew_pad, vnew_pad, z_rest,
      *([pool_k] * pages_per_step), *([pool_v] * pages_per_step))


def _lru_kernel(xr_ref, gr_ref, cp_ref, hp_ref, cw_ref, cb_ref, wr_ref, br_ref, wg_ref, bg_ref, lam_ref,
                o_ref, hl_ref, cn_ref, ext_ref, hc_ref, a_ref, b_ref, *, tt, layer):
    t = pl.program_id(1)
    tail = CONV_W - 1
    base = SUBLANES
    width = xr_ref.shape[1]
    bw = width // LRU_BLOCKS
    vec = lambda ref: ref[layer:layer + 1, :]

    @pl.when(t == 0)
    def _():
        ext_ref[base - tail:base, :] = cp_ref[...]
        hc_ref[...] = hp_ref[...]

    x = xr_ref[...]
    ext_ref[base:base + tt, :] = x
    cw = cw_ref[...]
    xc = vec(cb_ref) + x * cw[tail:tail + 1, :]
    for j in range(tail):
        xc = xc + ext_ref[base - tail + j: base - tail + j + tt, :] * cw[j:j + 1, :]
    new_tail = x[tt - tail:, :]
    ext_ref[base - tail:base, :] = new_tail

    xcb = xc.astype(BF16)
    r_lin = jnp.concatenate([_dot(xcb[:, n * bw:(n + 1) * bw], wr_ref[n].astype(BF16))
                             for n in range(LRU_BLOCKS)], axis=1)
    g_lin = jnp.concatenate([_dot(xcb[:, n * bw:(n + 1) * bw], wg_ref[n].astype(BF16))
                             for n in range(LRU_BLOCKS)], axis=1)
    r = _sigmoid(r_lin + vec(br_ref))
    gi = _sigmoid(g_lin + vec(bg_ref))
    nl = -vec(lam_ref)
    softplus = jnp.maximum(nl, 0.0) + jnp.log1p(jnp.exp(-jnp.abs(nl)))
    log_a = (-LRU_C) * r * softplus
    a = jnp.exp(log_a)
    b = jnp.sqrt(jnp.tanh(-log_a) * (a * a + 1.0)) * (gi * xc)

    row = lax.broadcasted_iota(I32, (tt, width), 0) & (SUBLANES - 1)
    d = 1
    while d < SUBLANES:
        keep = row >= d
        a_s = jnp.where(keep, pltpu.roll(a, d, 0), 1.0)
        b_s = jnp.where(keep, pltpu.roll(b, d, 0), 0.0)
        b = a * b_s + b
        a = a * a_s
        d *= 2
    a_ref[...] = a
    b_ref[...] = b

    def group_body(gidx, h):
        off = pl.multiple_of(gidx * SUBLANES, SUBLANES)
        h8 = a_ref[pl.ds(off, SUBLANES), :] * h + b_ref[pl.ds(off, SUBLANES), :]
        b_ref[pl.ds(off, SUBLANES), :] = h8
        return h8[SUBLANES - 1:SUBLANES, :]

    h_last = lax.fori_loop(0, tt // SUBLANES, group_body, hc_ref[...])
    hc_ref[...] = h_last
    o_ref[...] = (b_ref[...] * _silu(gr_ref[...])).astype(o_ref.dtype)

    @pl.when(t == pl.num_programs(1) - 1)
    def _():
        hl_ref[...] = h_last
        cn_ref[...] = new_tail


def _lru(z_rest, conv_prev, h_prev, conv_w, conv_b, w_r, b_r, w_g, b_g, lam, layer, batch, t_len, tt, out_dtype):
    width = conv_w.shape[2]
    nt = t_len // tt
    tail = CONV_W - 1
    bw = width // LRU_BLOCKS
    whole = lambda a: pl.BlockSpec(a.shape, lambda b, t: (0,) * a.ndim)
    per_b = lambda b, t: (b, 0, 0)
    return pl.pallas_call(
        functools.partial(_lru_kernel, tt=tt, layer=layer),
        grid=(batch, nt),
        in_specs=[pl.BlockSpec((tt, width), lambda b, t: (b * nt + t, 1)),
                  pl.BlockSpec((tt, width), lambda b, t: (b * nt + t, 2)),
                  pl.BlockSpec((None, tail, width), per_b), pl.BlockSpec((None, 1, width), per_b),
                  pl.BlockSpec((None, CONV_W, width), lambda b, t: (layer, 0, 0)), whole(conv_b),
                  pl.BlockSpec((None, LRU_BLOCKS, bw, bw), lambda b, t: (layer, 0, 0, 0)), whole(b_r),
                  pl.BlockSpec((None, LRU_BLOCKS, bw, bw), lambda b, t: (layer, 0, 0, 0)), whole(b_g),
                  whole(lam)],
        out_specs=[pl.BlockSpec((tt, width), lambda b, t: (b * nt + t, 0)),
                   pl.BlockSpec((None, 1, width), per_b), pl.BlockSpec((None, tail, width), per_b)],
        out_shape=[jax.ShapeDtypeStruct((batch * t_len, width), out_dtype),
                   jax.ShapeDtypeStruct((batch, 1, width), F32),
                   jax.ShapeDtypeStruct((batch, tail, width), F32)],
        scratch_shapes=[pltpu.VMEM((SUBLANES + tt, width), F32), pltpu.VMEM((1, width), F32),
                        pltpu.VMEM((tt, width), F32), pltpu.VMEM((tt, width), F32)],
        compiler_params=_cparams(2),
        name="rglru",
    )(z_rest, z_rest, conv_prev, h_prev, conv_w, conv_b, w_r, b_r, w_g, b_g, lam)


def _mem_attn_kernel(qm_ref, gm_ref, mk_ref, mv_ref, o_ref):
    width = qm_ref.shape[1]
    hd = width // MEM_HEADS
    scale = hd ** -0.5
    for h in range(MEM_HEADS):
        sl = slice(h * hd, (h + 1) * hd)
        s = _dot_nt(qm_ref[:, sl].astype(BF16), mk_ref[:, sl].astype(BF16)) * scale
        p = jnp.exp(s - jnp.max(s, axis=1, keepdims=True))
        l = jnp.sum(p, axis=1, keepdims=True)
        o = _dot(p.astype(BF16), mv_ref[:, sl].astype(BF16)) / l
        o_ref[:, sl] = (o * _silu(gm_ref[:, sl])).astype(o_ref.dtype)


def _mem_attention(z_rest, mk, mv, batch, t_len, tm, out_dtype):
    width = mk.shape[2]
    mem = mk.shape[1]
    nt = t_len // tm
    return pl.pallas_call(
        _mem_attn_kernel,
        grid=(batch, nt),
        in_specs=[pl.BlockSpec((tm, width), lambda b, t: (b * nt + t, 3)),
                  pl.BlockSpec((tm, width), lambda b, t: (b * nt + t, 4)),
                  pl.BlockSpec((None, mem, width), lambda b, t: (b, 0, 0)),
                  pl.BlockSpec((None, mem, width), lambda b, t: (b, 0, 0))],
        out_specs=pl.BlockSpec((tm, width), lambda b, t: (b * nt + t, 0)),
        out_shape=jax.ShapeDtypeStruct((batch * t_len, width), out_dtype),
        compiler_params=_cparams(2),
        name="mem_attention",
    )(z_rest, z_rest, mk, mv)


def _merge_kernel(ba_ref, bl_ref, bm_ref, ga_ref, gl_ref, gm_ref, wa_ref, wl_ref, wm_ref, o_ref, wbf_ref):
    @pl.when(pl.program_id(1) == 0)
    def _():
        for b, w_ref in enumerate((wa_ref, wl_ref, wm_ref)):
            _cast_weight(wbf_ref.at[b], w_ref)

    out = _sigmoid(ga_ref[...]) * _dot(ba_ref[...].astype(BF16), wbf_ref[0])
    out = out + _sigmoid(gl_ref[...]) * _dot(bl_ref[...].astype(BF16), wbf_ref[1])
    out = out + _sigmoid(gm_ref[...]) * _dot(bm_ref[...].astype(BF16), wbf_ref[2])
    o_ref[...] = out.astype(BF16)


def _merge(branches, z_rest, w_branch, layer, tm, tn):
    m, bw = branches[0].shape
    d = w_branch.shape[3]
    gate_col0 = 5 * bw // tn
    br_spec = pl.BlockSpec((tm, bw), lambda j, i: (i, 0))

    def gate_spec(b):
        return pl.BlockSpec((tm, tn), lambda j, i: (i, gate_col0 + b * (d // tn) + j))

    def w_spec(b):
        return pl.BlockSpec((None, None, bw, tn), lambda j, i: (layer, b, 0, j))

    return pl.pallas_call(
        _merge_kernel,
        grid=(d // tn, m // tm),
        in_specs=[br_spec] * N_BRANCH + [gate_spec(b) for b in range(N_BRANCH)]
                 + [w_spec(b) for b in range(N_BRANCH)],
        out_specs=pl.BlockSpec((tm, tn), lambda j, i: (i, j)),
        out_shape=jax.ShapeDtypeStruct((m, d), BF16),
        scratch_shapes=[pltpu.VMEM((N_BRANCH, bw, tn), BF16)],
        compiler_params=_cparams(2),
        name="merge",
    )(*branches, z_rest, z_rest, z_rest, w_branch, w_branch, w_branch)


def _rope_tables(pos, reps):
    def tab(dim):
        half = dim // 2
        freq = ROPE_THETA ** (-jnp.arange(half, dtype=F32) / half)
        ang = pos.astype(F32)[:, None] * freq[None, :]
        cos, sin = jnp.cos(ang), jnp.sin(ang)
        n = LANES // dim
        c = jnp.tile(jnp.concatenate([cos, cos], axis=1), (reps, n))
        s = jnp.tile(jnp.concatenate([-sin, sin], axis=1), (reps, n))
        return c, s
    cq, sq = tab(HEAD_DIM)
    ci, si = tab(IDX_DIM)
    return cq, sq, ci, si


def _pick_tile(m, pref):
    t = min(m, pref)
    while m % t:
        t //= 2
    return t


def kernel(x_prompt, x_sample, mem_prompt, cache_k, cache_v, cache_idx_k, cache_mem_k, cache_mem_v, state_conv, state_h, page_table, norm_g, w_in, conv_w, conv_b, w_rgate, b_rgate, w_igate, b_igate, lru_lambda, mem_norm_g, w_mem_kv, w_branch, w_out, final_norm_g):
    bp, seq, d = x_prompt.shape
    bs, t_new, _ = x_sample.shape
    depth = w_in.shape[0]
    n_pool, page = cache_k.shape[1], cache_k.shape[2]
    n_pages = page_table.shape[1]
    past = n_pages * page
    mem_tokens = mem_prompt.shape[1]
    lru_w = conv_w.shape[2]
    mem_w = w_mem_kv.shape[2] // 2
    mem_hd = mem_w // MEM_HEADS
    rest0 = ATTN_W + 2 * KV_W + QI_W + IDX_DIM + IDX_HEADS
    n_rest = w_in.shape[2] - rest0
    topk_s = min(INDEX_TOPK, (past + t_new) // 4)
    pages_per_step = _pick_tile(n_pages, 16)

    tabs_p = _rope_tables(jnp.arange(seq), bp)
    tabs_s = _rope_tables(past + jnp.arange(t_new), bs)

    mp, ms = bp * seq, bs * t_new
    xp = x_prompt.reshape(mp, d)
    xs = x_sample.reshape(ms, d)
    mem2d = mem_prompt.reshape(bp * mem_tokens, d)
    conv0 = jnp.zeros((bp, CONV_W - 1, lru_w), F32)
    h0 = jnp.zeros((bp, 1, lru_w), F32)
    pool_k = cache_k.reshape(depth, n_pool, page * KV_HEADS, HEAD_DIM)
    pool_v = cache_v.reshape(depth, n_pool, page * KV_HEADS, HEAD_DIM)
    w_in_t = jnp.swapaxes(w_in, 1, 2)
    pool_ik_t = jnp.swapaxes(cache_idx_k, 2, 3)
    mem_k_s = cache_mem_k.reshape(depth, bs, mem_tokens, mem_w)
    mem_v_s = cache_mem_v.reshape(depth, bs, mem_tokens, mem_w)
    h_s0 = state_h.reshape(depth, bs, 1, lru_w)
    lru_w_args = (conv_w, conv_b, w_rgate, b_rgate, w_igate, b_igate, lru_lambda)

    tm_p = _pick_tile(mp, 256)
    tm_mm = _pick_tile(mp, 1024)
    tm_out = _pick_tile(mp, 512)

    st_p, st_s, mk_list, mv_list = [], [], [], []
    for l in range(depth):
        m_kv = _norm_matmul(mem2d, mem_norm_g, w_mem_kv, l, _pick_tile(bp * mem_tokens, 512), 1024)
        mk = m_kv[:, :mem_w].reshape(bp, mem_tokens, mem_w)
        mv = m_kv[:, mem_w:].reshape(bp, mem_tokens, mem_w)
        mk_list.append(mk.reshape(bp, mem_tokens, MEM_HEADS, mem_hd))
        mv_list.append(mv.reshape(bp, mem_tokens, MEM_HEADS, mem_hd))

        xn, k, v, ki, kb, kib, q_t, qi_t, wi_t, vb_t = _proj_attn(xp, norm_g, w_in_t, tabs_p, l, tm_p, True)
        z_rest = _matmul(xn, w_in_t, l, rest0, n_rest, tm_mm, 1024, True)
        br_att = _prompt_attention(qi_t, wi_t, kib, q_t, kb, vb_t, z_rest, bp, seq)
        br_lru, h_last, conv_new = _lru(z_rest, conv0, h0, *lru_w_args, l, bp, seq, _pick_tile(seq, 256), BF16)
        br_mem = _mem_attention(z_rest, mk, mv, bp, seq, _pick_tile(seq, 512), BF16)
        merged = _merge([br_att, br_lru, br_mem], z_rest, w_branch, l, tm_out, 512)
        xp = _matmul(merged, w_out, l, 0, d, tm_out, 1024, False, residual=xp)
        st_p.append((k.reshape(bp, seq, KV_HEADS, HEAD_DIM), v.reshape(bp, seq, KV_HEADS, HEAD_DIM),
                     ki.reshape(bp, seq, IDX_DIM), conv_new, h_last.reshape(bp, lru_w)))

        xn, k, v, ki, kb, kib, q, qi, wi, vb = _proj_attn(xs, norm_g, w_in_t, tabs_s, l, ms, False)
        z_rest = _matmul(xn, w_in_t, l, rest0, n_rest, ms, 1024, True)
        qs = qi.reshape(ms * IDX_HEADS, IDX_DIM)
        ws = wi.reshape(ms * IDX_HEADS, 1)
        pad_rows = lambda a: jnp.pad(a.reshape(bs, t_new, a.shape[1]), ((0, 0), (0, LANES - t_new), (0, 0)))
        sc = _sample_index(page_table, qs, ws, pool_ik_t, l, bs, t_new, pages_per_step)
        bias, bias_new = _sample_select(sc, qs, ws, pad_rows(kib), bs, t_new, topk_s)
        br_att = _sample_attention(page_table, q, bias, bias_new, pad_rows(kb), pad_rows(vb), z_rest,
                                   pool_k, pool_v, l, bs, t_new, pages_per_step)
        br_lru, h_last, conv_new = _lru(z_rest, state_conv[l], h_s0[l], *lru_w_args, l, bs, t_new, t_new, F32)
        br_mem = _mem_attention(z_rest, mem_k_s[l], mem_v_s[l], bs, t_new, t_new, F32)
        merged = _merge([br_att, br_lru, br_mem], z_rest, w_branch, l, ms, 512)
        xs = _matmul(merged, w_out, l, 0, d, ms, 1024, False, residual=xs)
        st_s.append((k.reshape(bs, t_new, KV_HEADS, HEAD_DIM), v.reshape(bs, t_new, KV_HEADS, HEAD_DIM),
                     ki.reshape(bs, t_new, IDX_DIM), conv_new, h_last.reshape(bs, lru_w)))

    g_f = final_norm_g.reshape(1, d)
    y_prompt = _rmsnorm(xp, g_f, _pick_tile(mp, 512)).reshape(bp, seq, d)
    y_sample = _rmsnorm(xs, g_f, ms).reshape(bs, t_new, d)
    k_p, v_p, ik_p, conv_p, h_p = [jnp.stack(t) for t in zip(*st_p)]
    k_s, v_s, ik_s, conv_s, h_s = [jnp.stack(t) for t in zip(*st_s)]
    return (y_prompt, y_sample, k_p, v_p, ik_p, conv_p, h_p, jnp.stack(mk_list), jnp.stack(mv_list),
            k_s, v_s, ik_s, conv_s, h_s)
```

```python
import functools

import jax
import jax.numpy as jnp
from jax import lax
from jax.experimental import pallas as pl
from jax.experimental.pallas import tpu as pltpu

F32 = jnp.float32
BF16 = jnp.bfloat16
I32 = jnp.int32

N_HEADS = 8
HEAD_DIM = 128
KV_HEADS = 2
HEADS_PER_KV = N_HEADS // KV_HEADS
IDX_HEADS = 8
IDX_DIM = 64
INDEX_TOPK = 256
LRU_BLOCKS = 8
CONV_W = 4
LRU_C = 8.0
MEM_HEADS = 4
N_BRANCH = 3
ROPE_THETA = 10000.0
NORM_EPS = 1e-6

ATTN_W = N_HEADS * HEAD_DIM
KV_W = KV_HEADS * HEAD_DIM
QI_W = IDX_HEADS * IDX_DIM

LANES = 128
SUBLANES = 8
VMEM_LIMIT = 52 * 1024 * 1024
CAST_ROWS = 256
COUNT_ROWS = 8 * SUBLANES
LOG2E = 1.4426950408889634
QK_SCALE = HEAD_DIM ** -0.5 * LOG2E
SUM_ROWS = 16

NEG_BIG = -1e30
INT_MIN = -(2 ** 31)
I16 = jnp.int16
HALF_BITS = 16
HALF_MASK = 2 ** HALF_BITS - 1
HALF_MIN = -(2 ** (HALF_BITS - 1))


def _cparams(n_axes):
    return pltpu.CompilerParams(dimension_semantics=("arbitrary",) * n_axes,
                                vmem_limit_bytes=VMEM_LIMIT)


def _sigmoid(x):
    return 1.0 / (1.0 + jnp.exp(-x))


def _silu(x):
    return x * _sigmoid(x)


def _dot(a, b):
    return jnp.dot(a, b, preferred_element_type=F32)


def _dot_nt(a, b):
    return lax.dot_general(a, b, (((1,), (1,)), ((), ())), preferred_element_type=F32)


def _rms(x, g):
    ms = jnp.mean(x * x, axis=-1, keepdims=True)
    return (x * lax.rsqrt(ms + NORM_EPS)) * g


def _cast_weight(dst_ref, src_ref):
    k = dst_ref.shape[0]
    for r in range(0, k, CAST_ROWS):
        rows = slice(r, min(r + CAST_ROWS, k))
        dst_ref[rows, :] = src_ref[rows, :].astype(BF16)


def _proj_attn_kernel(x_ref, g_ref, w_ref, cq_ref, sq_ref, ci_ref, si_ref,
                      xn_ref, k_ref, v_ref, ki_ref, kb_ref, kib_ref, q_ref, qi_ref, wi_ref, vb_ref,
                      wbf_ref, *, layer, transposed):
    @pl.when(pl.program_id(0) == 0)
    def _():
        _cast_weight(wbf_ref, w_ref)

    xnb = _rms(x_ref[...], g_ref[layer:layer + 1, :]).astype(BF16)
    xn_ref[...] = xnb
    z = _dot_nt(xnb, wbf_ref[...])

    cq, sq, ci, si = cq_ref[...], sq_ref[...], ci_ref[...], si_ref[...]
    lane = lax.broadcasted_iota(I32, cq.shape, 1)
    first_half = (lane & (IDX_DIM // 2)) == 0

    def rope128(t):
        return t * cq + pltpu.roll(t, HEAD_DIM // 2, 1) * sq

    def rope64(t):
        rot = jnp.where(first_half, pltpu.roll(t, LANES - IDX_DIM // 2, 1),
                        pltpu.roll(t, IDX_DIM // 2, 1))
        return t * ci + rot * si

    def tile(c):
        return z[:, c * LANES:(c + 1) * LANES]

    for h in range(N_HEADS):
        qh = rope128(tile(h)) * QK_SCALE
        if transposed:
            q_ref[h * HEAD_DIM:(h + 1) * HEAD_DIM, :] = qh.T.astype(BF16)
        else:
            q_ref[:, h * HEAD_DIM:(h + 1) * HEAD_DIM] = qh
    c0 = ATTN_W // LANES
    for h in range(KV_HEADS):
        sl = slice(h * HEAD_DIM, (h + 1) * HEAD_DIM)
        kr = rope128(tile(c0 + h))
        k_ref[:, sl] = kr
        kb_ref[:, sl] = kr.astype(BF16)
    c0 += KV_W // LANES
    for h in range(KV_HEADS):
        sl = slice(h * HEAD_DIM, (h + 1) * HEAD_DIM)
        vh = tile(c0 + h)
        v_ref[:, sl] = vh
        if transposed:
            vb_ref[sl, :] = vh.T.astype(BF16)
        else:
            vb_ref[:, sl] = vh.astype(BF16)
    c0 += KV_W // LANES
    for c in range(QI_W // LANES):
        qc = rope64(tile(c0 + c))
        if transposed:
            qi_ref[c * LANES:(c + 1) * LANES, :] = qc.T.astype(BF16)
        else:
            qi_ref[:, c * LANES:(c + 1) * LANES] = qc.astype(BF16)
    c0 += QI_W // LANES
    last = tile(c0)
    kir = rope64(last)[:, :IDX_DIM]
    ki_ref[...] = kir
    kib_ref[...] = kir.astype(BF16)
    if transposed:
        wi_ref[...] = last.T[IDX_DIM:IDX_DIM + IDX_HEADS, :] * (IDX_HEADS ** -0.5)
    else:
        wi_ref[...] = last[:, IDX_DIM:IDX_DIM + IDX_HEADS] * (IDX_HEADS ** -0.5)


def _proj_attn(x, norm_g, w_in_t, tabs, layer, tm, transposed):
    m, d = x.shape
    na = ATTN_W + 2 * KV_W + QI_W + LANES
    row = lambda i: (i, 0)
    col = lambda i: (0, i)
    const = lambda i: (0, 0)

    def tok(width, dtype):
        return ((m, width), dtype, pl.BlockSpec((tm, width), row))

    def feat(width, dtype):
        return ((width, m), dtype, pl.BlockSpec((width, tm), col))

    outs = [tok(d, BF16), tok(KV_W, F32), tok(KV_W, F32), tok(IDX_DIM, F32), tok(KV_W, BF16), tok(IDX_DIM, BF16)]
    if transposed:
        outs += [feat(ATTN_W, BF16), feat(QI_W, BF16), feat(IDX_HEADS, F32), feat(KV_W, BF16)]
    else:
        outs += [tok(ATTN_W, F32), tok(QI_W, BF16), tok(IDX_HEADS, F32), tok(KV_W, BF16)]
    return pl.pallas_call(
        functools.partial(_proj_attn_kernel, layer=layer, transposed=transposed),
        grid=(m // tm,),
        in_specs=[pl.BlockSpec((tm, d), row), pl.BlockSpec(norm_g.shape, const),
                  pl.BlockSpec((None, na, d), lambda i: (layer, 0, 0), pipeline_mode=pl.Buffered(1))]
                 + [pl.BlockSpec((tm, LANES), row)] * 4,
        out_specs=[o[2] for o in outs],
        out_shape=[jax.ShapeDtypeStruct(o[0], o[1]) for o in outs],
        scratch_shapes=[pltpu.VMEM((na, d), BF16)],
        compiler_params=_cparams(1),
        name="proj_attn",
    )(x, norm_g, w_in_t, *tabs)


def _mm_kernel(*refs, w_transposed, residual):
    x_ref, w_ref = refs[0], refs[1]
    r_ref = refs[2] if residual else None
    o_ref, wbf_ref = refs[-2], refs[-1]

    @pl.when(pl.program_id(1) == 0)
    def _():
        _cast_weight(wbf_ref, w_ref.at[0] if w_transposed else w_ref)

    out = _dot_nt(x_ref[...], wbf_ref[...]) if w_transposed else _dot(x_ref[...], wbf_ref[...])
    if residual:
        out = r_ref[...] + out
    o_ref[...] = out


def _matmul(x, w, layer, col0, n, tm, tn, w_transposed, residual=None):
    m, k = x.shape
    assert n % tn == 0 and m % tm == 0
    if w_transposed:
        assert col0 % SUBLANES == 0
        w_spec = pl.BlockSpec((pl.Element(1), pl.Element(tn), pl.Element(k)),
                              lambda j, i: (layer, (col0 // SUBLANES + j * (tn // SUBLANES)) * SUBLANES, 0))
        w_scratch = pltpu.VMEM((tn, k), BF16)
    else:
        assert col0 % tn == 0
        w_spec = pl.BlockSpec((None, k, tn), lambda j, i: (layer, 0, col0 // tn + j))
        w_scratch = pltpu.VMEM((k, tn), BF16)
    in_specs = [pl.BlockSpec((tm, k), lambda j, i: (i, 0)), w_spec]
    args = [x, w]
    if residual is not None:
        in_specs.append(pl.BlockSpec((tm, tn), lambda j, i: (i, j)))
        args.append(residual)
    return pl.pallas_call(
        functools.partial(_mm_kernel, w_transposed=w_transposed, residual=residual is not None),
        grid=(n // tn, m // tm),
        in_specs=in_specs,
        out_specs=pl.BlockSpec((tm, tn), lambda j, i: (i, j)),
        out_shape=jax.ShapeDtypeStruct((m, n), F32),
        scratch_shapes=[w_scratch],
        compiler_params=_cparams(2),
        name="matmul_residual" if residual is not None else "matmul",
    )(*args)


def _norm_mm_kernel(x_ref, g_ref, w_ref, o_ref, *, layer):
    xn = _rms(x_ref[...], g_ref[layer:layer + 1, :]).astype(BF16)
    o_ref[...] = _dot(xn, w_ref[...].astype(BF16))


def _norm_matmul(x, g, w, layer, tm, tn):
    m, k = x.shape
    n = w.shape[2]
    return pl.pallas_call(
        functools.partial(_norm_mm_kernel, layer=layer),
        grid=(n // tn, m // tm),
        in_specs=[pl.BlockSpec((tm, k), lambda j, i: (i, 0)), pl.BlockSpec(g.shape, lambda j, i: (0, 0)),
                  pl.BlockSpec((None, k, tn), lambda j, i: (layer, 0, j))],
        out_specs=pl.BlockSpec((tm, tn), lambda j, i: (i, j)),
        out_shape=jax.ShapeDtypeStruct((m, n), F32),
        compiler_params=_cparams(2),
        name="norm_matmul",
    )(x, g, w)


def _rmsnorm_kernel(x_ref, g_ref, o_ref):
    o_ref[...] = _rms(x_ref[...], g_ref[...])


def _rmsnorm(x, g, tm):
    m, d = x.shape
    return pl.pallas_call(
        _rmsnorm_kernel,
        grid=(m // tm,),
        in_specs=[pl.BlockSpec((tm, d), lambda i: (i, 0)), pl.BlockSpec((1, d), lambda i: (0, 0))],
        out_specs=pl.BlockSpec((tm, d), lambda i: (i, 0)),
        out_shape=jax.ShapeDtypeStruct((m, d), F32),
        compiler_params=_cparams(1),
        name="final_rmsnorm",
    )(x, g)


def _sortable_key(score):
    bits = pltpu.bitcast(score, I32)
    return bits ^ ((bits >> 31) & jnp.int32(0x7FFFFFFF))


def _kth_largest_key(count_ge, shape, topk, bits=32):
    zero = jnp.zeros(shape, I32)
    ans = jnp.where(count_ge(zero) >= topk, zero, jnp.full(shape, -(2 ** (bits - 1)), I32))

    def bit_body(j, ans):
        cand = ans | lax.shift_left(jnp.int32(1), bits - 2 - j)
        return jnp.where(count_ge(cand) >= topk, cand, ans)

    return lax.fori_loop(0, bits - 1, bit_body, ans)


def _tie_cutoff(count_tie_lt, need, shape, idx_bits):
    def bit_body(j, cut):
        cand = cut | lax.shift_left(jnp.int32(1), idx_bits - 1 - j)
        return jnp.where(count_tie_lt(cand) < need, cand, cut)

    return lax.fori_loop(0, idx_bits, bit_body, jnp.zeros(shape, I32))


def _prompt_attn_kernel(qi_ref, wi_ref, kib_ref, q_ref, kb_ref, vb_ref, ga_ref, o_ref,
                        key_ref, hi_ref, lo_ref, bias_ref, acc_ref, *, tq, kc, topk, seq):
    i = pl.program_id(1)
    nch = ((i + 1) * tq + kc - 1) // kc
    key_pos = lax.broadcasted_iota(I32, (kc, tq), 0)
    q_pos = i * tq + lax.broadcasted_iota(I32, (kc, tq), 1)
    idx_bits = max(1, (seq - 1).bit_length())
    row1 = (1, tq)

    def chunk(ref, c):
        return ref[pl.ds(pl.multiple_of(c * kc, kc), kc), :]

    wi = wi_ref[...] * (IDX_DIM ** -0.5)

    def score_body(c, carry):
        kic = chunk(kib_ref, c)
        acc = jnp.zeros((kc, tq), F32)
        for h in range(IDX_HEADS):
            s = _dot(kic, qi_ref[h * IDX_DIM:(h + 1) * IDX_DIM, :])
            acc = acc + jnp.maximum(s, 0.0) * wi[h:h + 1, :]
        key = jnp.where(c * kc + key_pos <= q_pos, _sortable_key(acc), INT_MIN)
        key_ref[pl.ds(pl.multiple_of(c * kc, kc), kc), :] = key
        hi_ref[pl.ds(pl.multiple_of(c * kc, kc), kc), :] = (key >> HALF_BITS).astype(I16)
        return carry

    lax.fori_loop(0, nch, score_body, 0)

    def count16_ge(ref, cand):
        c16 = cand.astype(I16)

        def body(c, acc):
            ind = jnp.where(chunk(ref, c) >= c16, jnp.ones((), I16), jnp.zeros((), I16))
            parts = [ind[r * COUNT_ROWS:(r + 1) * COUNT_ROWS] for r in range(kc // COUNT_ROWS)]
            while len(parts) > 1:
                parts = [a + b for a, b in zip(parts[::2], parts[1::2])]
            return acc + parts[0]

        acc = lax.fori_loop(0, nch, body, jnp.zeros((COUNT_ROWS, tq), I16))
        return jnp.sum(acc.astype(I32), axis=0, keepdims=True)

    def kth_largest():
        t_hi = _kth_largest_key(lambda cand: count16_ge(hi_ref, cand), row1, topk, bits=HALF_BITS)
        need_lo = topk - count16_ge(hi_ref, t_hi + 1)

        def low_body(c, carry):
            k = chunk(key_ref, c)
            lo = jnp.where((k >> HALF_BITS) == t_hi, (k & HALF_MASK) + HALF_MIN, HALF_MIN)
            lo_ref[pl.ds(pl.multiple_of(c * kc, kc), kc), :] = lo.astype(I16)
            return carry

        lax.fori_loop(0, nch, low_body, 0)
        t_lo = _kth_largest_key(lambda cand: count16_ge(lo_ref, cand), row1, need_lo, bits=HALF_BITS)
        thr = lax.shift_left(t_hi, HALF_BITS) | (t_lo - HALF_MIN)
        return jnp.where(t_hi == HALF_MIN, INT_MIN + 1, jnp.maximum(thr, INT_MIN + 1))

    def chunk_count(pred):
        def body(c, acc):
            ind = jnp.where(pred(chunk(key_ref, c), c * kc + key_pos), 1.0, 0.0)
            return acc + jnp.sum(ind.reshape(kc // COUNT_ROWS, COUNT_ROWS, tq), axis=0)
        acc = lax.fori_loop(0, nch, body, jnp.zeros((COUNT_ROWS, tq), F32))
        return jnp.sum(acc, axis=0, keepdims=True)

    def count_ge(cand):
        return chunk_count(lambda k, pos: k >= cand)

    def write_bias(keep):
        def body(c, carry):
            k = chunk(key_ref, c)
            bias_ref[pl.ds(pl.multiple_of(c * kc, kc), kc), :] = jnp.where(keep(k, c * kc + key_pos), 0.0, NEG_BIG)
            return carry
        lax.fori_loop(0, nch, body, 0)

    need_search = (i + 1) * tq > topk
    thr = lax.cond(need_search, kth_largest, lambda: jnp.full(row1, INT_MIN + 1, I32))
    write_bias(lambda k, pos: k >= thr)

    @pl.when(jnp.max(count_ge(thr)) > topk)
    def _():
        need = topk - count_ge(thr + 1)
        count_tie_lt = lambda cut: chunk_count(lambda k, pos: jnp.where(k == thr, pos, seq) < cut)
        cut = _tie_cutoff(count_tie_lt, need, row1, idx_bits)
        write_bias(lambda k, pos: (k > thr) | ((k == thr) & (pos <= cut)))

    qw = HEADS_PER_KV * tq
    q4 = [jnp.concatenate([q_ref[(g * HEADS_PER_KV + hh) * HEAD_DIM:(g * HEADS_PER_KV + hh + 1) * HEAD_DIM, :]
                           for hh in range(HEADS_PER_KV)], axis=1) for g in range(KV_HEADS)]
    ones_rows = jnp.ones((SUM_ROWS, kc), BF16)
    acc_ref[...] = jnp.zeros(acc_ref.shape, F32)

    def attn_body(c, m_carry):
        off = pl.multiple_of(c * kc, kc)
        bias = chunk(bias_ref, c)
        bias4 = jnp.concatenate([bias] * HEADS_PER_KV, axis=1)
        out = []
        for g in range(KV_HEADS):
            m_prev = m_carry[g]
            hs = slice(g * HEAD_DIM, (g + 1) * HEAD_DIM)
            s = _dot(kb_ref[pl.ds(off, kc), hs], q4[g]) + bias4
            m_next = jnp.maximum(m_prev, jnp.max(s, axis=0, keepdims=True))
            p = jnp.exp2(s - m_next).astype(BF16)
            alpha = jnp.exp2(m_prev - m_next)
            v_aug = jnp.concatenate([vb_ref[hs, pl.ds(off, kc)], ones_rows], axis=0)
            acc_ref[g] = acc_ref[g] * alpha + _dot(v_aug, p)
            out.append(m_next)
        return tuple(out)

    init = tuple(jnp.full((1, qw), NEG_BIG, F32) for _ in range(KV_HEADS))
    lax.fori_loop(0, nch, attn_body, init)
    for g in range(KV_HEADS):
        acc = acc_ref[g]
        o_t = acc[:HEAD_DIM] / acc[HEAD_DIM:HEAD_DIM + 1]
        for hh in range(HEADS_PER_KV):
            sl = slice((g * HEADS_PER_KV + hh) * HEAD_DIM, (g * HEADS_PER_KV + hh + 1) * HEAD_DIM)
            o_ref[:, sl] = (o_t[:, hh * tq:(hh + 1) * tq].T * _silu(ga_ref[:, sl])).astype(BF16)


def _prompt_attention(qi_t, wi_t, kib, q_t, kb, vb_t, z_rest, batch, seq, tq=128):
    nq = seq // tq
    kc = _pick_tile(seq, 512)
    topk = min(INDEX_TOPK, seq // 4)
    blk_t = lambda b, i: (0, b * nq + i)
    blk = lambda b, i: (b * nq + i, 0)
    return pl.pallas_call(
        functools.partial(_prompt_attn_kernel, tq=tq, kc=kc, topk=topk, seq=seq),
        grid=(batch, nq),
        in_specs=[pl.BlockSpec((QI_W, tq), blk_t), pl.BlockSpec((IDX_HEADS, tq), blk_t),
                  pl.BlockSpec((seq, IDX_DIM), lambda b, i: (b, 0)), pl.BlockSpec((ATTN_W, tq), blk_t),
                  pl.BlockSpec((seq, KV_W), lambda b, i: (b, 0)), pl.BlockSpec((KV_W, seq), lambda b, i: (0, b)),
                  pl.BlockSpec((tq, ATTN_W), blk)],
        out_specs=pl.BlockSpec((tq, ATTN_W), blk),
        out_shape=jax.ShapeDtypeStruct((batch * seq, ATTN_W), BF16),
        scratch_shapes=[pltpu.VMEM((seq, tq), I32), pltpu.VMEM((seq, tq), I16), pltpu.VMEM((seq, tq), I16),
                        pltpu.VMEM((seq, tq), F32),
                        pltpu.VMEM((KV_HEADS, HEAD_DIM + SUM_ROWS, HEADS_PER_KV * tq), F32)],
        compiler_params=_cparams(2),
        name="prompt_attention",
    )(qi_t, wi_t, kib, q_t, kb, vb_t, z_rest)


def _sample_index_kernel(pt_ref, qs_ref, ws_ref, *rest, pages_per_step, t_new):
    page_refs, o_ref = rest[:pages_per_step], rest[pages_per_step]
    qs = qs_ref[...]
    ws = ws_ref[...] * (IDX_DIM ** -0.5)
    page = page_refs[0].shape[1]
    wsb = jnp.broadcast_to(ws, (t_new * IDX_HEADS, page))
    for j in range(pages_per_step):
        kp = page_refs[j][...].astype(BF16)
        s = jnp.maximum(_dot(qs, kp), 0.0) * wsb
        o_ref[:, j * page:(j + 1) * page] = jnp.sum(s.reshape(t_new, IDX_HEADS, page), axis=1)


def _sample_index(page_table, qs, ws, pool_ik_t, layer, dec_batch, t_new, pages_per_step):
    n_pages = page_table.shape[1]
    page = pool_ik_t.shape[3]
    steps = n_pages // pages_per_step
    rows = t_new * IDX_HEADS

    def page_map(j):
        return lambda b, s, pt: (layer, pt[b, s * pages_per_step + j], 0, 0)

    grid_spec = pltpu.PrefetchScalarGridSpec(
        num_scalar_prefetch=1,
        grid=(dec_batch, steps),
        in_specs=[pl.BlockSpec((rows, IDX_DIM), lambda b, s, pt: (b, 0)),
                  pl.BlockSpec((rows, 1), lambda b, s, pt: (b, 0))]
                 + [pl.BlockSpec((None, None, IDX_DIM, page), page_map(j)) for j in range(pages_per_step)],
        out_specs=pl.BlockSpec((t_new, pages_per_step * page), lambda b, s, pt: (b, s)),
    )
    return pl.pallas_call(
        functools.partial(_sample_index_kernel, pages_per_step=pages_per_step, t_new=t_new),
        grid_spec=grid_spec,
        out_shape=jax.ShapeDtypeStruct((dec_batch * t_new, n_pages * page), F32),
        compiler_params=_cparams(2),
        name="sample_index",
    )(page_table, qs, ws, *([pool_ik_t] * pages_per_step))


def _sample_select_kernel(sc_ref, qs_ref, ws_ref, kn_ref, bias_ref, bias_new_ref, key_ref, keyn_ref,
                          *, dec_batch, t_new, past, topk):
    rows = dec_batch * t_new
    hrows = t_new * IDX_HEADS
    ws = ws_ref[...] * (IDX_DIM ** -0.5)
    lane_n = lax.broadcasted_iota(I32, (rows, LANES), 1)
    row_n = lax.broadcasted_iota(I32, (rows, LANES), 0) % t_new
    s_new = []
    for b in range(dec_batch):
        hs = slice(b * hrows, (b + 1) * hrows)
        s = jnp.maximum(_dot_nt(qs_ref[hs, :], kn_ref[b]), 0.0) * jnp.broadcast_to(ws[hs, :], (hrows, LANES))
        s_new.append(jnp.sum(s.reshape(t_new, IDX_HEADS, LANES), axis=1))
    s_new = jnp.concatenate(s_new, axis=0) + 0.0
    keyn_ref[...] = jnp.where(lane_n <= row_n, _sortable_key(s_new), INT_MIN)
    key_ref[...] = _sortable_key(sc_ref[...] + 0.0)
    pos = lax.broadcasted_iota(I32, (rows, past), 1)
    idx_bits = (past + LANES - 1).bit_length()
    col1 = (rows, 1)

    def count(pred):
        a = jnp.sum(jnp.where(pred(key_ref[...], pos), 1.0, 0.0), axis=1, keepdims=True)
        b = jnp.sum(jnp.where(pred(keyn_ref[...], past + lane_n), 1.0, 0.0), axis=1, keepdims=True)
        return a + b

    count_ge = lambda cand: count(lambda k, p: k >= cand)
    thr = jnp.maximum(_kth_largest_key(count_ge, col1, topk), INT_MIN + 1)
    bias_ref[...] = jnp.where(key_ref[...] >= thr, 0.0, NEG_BIG)
    bias_new_ref[...] = jnp.where(keyn_ref[...] >= thr, 0.0, NEG_BIG)

    @pl.when(jnp.max(count_ge(thr)) > topk)
    def _():
        need = topk - count_ge(thr + 1)
        big = jnp.int32(2 ** idx_bits)
        count_tie_lt = lambda cut: count(lambda k, p: jnp.where(k == thr, p, big) < cut)
        cut = _tie_cutoff(count_tie_lt, need, col1, idx_bits)
        keep = lambda k, p: (k > thr) | ((k == thr) & (p <= cut))
        bias_ref[...] = jnp.where(keep(key_ref[...], pos), 0.0, NEG_BIG)
        bias_new_ref[...] = jnp.where(keep(keyn_ref[...], past + lane_n), 0.0, NEG_BIG)


def _sample_select(sc, qs, ws, knew_pad, dec_batch, t_new, topk):
    rows, past = sc.shape
    whole = lambda a: pl.BlockSpec(a.shape, lambda i: (0,) * a.ndim)
    out_shape = [jax.ShapeDtypeStruct((rows, past), F32), jax.ShapeDtypeStruct((rows, LANES), F32)]
    return pl.pallas_call(
        functools.partial(_sample_select_kernel, dec_batch=dec_batch, t_new=t_new, past=past, topk=topk),
        grid=(1,),
        in_specs=[whole(sc), whole(qs), whole(ws), whole(knew_pad)],
        out_specs=[whole(o) for o in out_shape],
        out_shape=out_shape,
        scratch_shapes=[pltpu.VMEM((rows, past), I32), pltpu.VMEM((rows, LANES), I32)],
        compiler_params=_cparams(1),
        name="sample_select",
    )(sc, qs, ws, knew_pad)


def _sample_attn_kernel(pt_ref, q_ref, bias_ref, biasn_ref, kn_ref, vn_ref, ga_ref, *rest,
                        pages_per_step, t_new):
    k_refs = rest[:pages_per_step]
    v_refs = rest[pages_per_step:2 * pages_per_step]
    o_ref, m_ref, l_ref, acc_ref = rest[2 * pages_per_step:]
    step = pl.program_id(1)
    page = k_refs[0].shape[0] // KV_HEADS

    @pl.when(step == 0)
    def _():
        m_ref[...] = jnp.full(m_ref.shape, NEG_BIG, F32)
        l_ref[...] = jnp.zeros(l_ref.shape, F32)
        acc_ref[...] = jnp.zeros(acc_ref.shape, F32)

    def q_group(g):
        return jnp.concatenate(
            [q_ref[:, (g * HEADS_PER_KV + hh) * HEAD_DIM:(g * HEADS_PER_KV + hh + 1) * HEAD_DIM]
             for hh in range(HEADS_PER_KV)], axis=0).astype(BF16)

    def head_rows(ref, g):
        return ref[pl.ds(g, page, stride=KV_HEADS), :].astype(BF16)

    def update(g, s, v_of):
        m_prev = m_ref[g]
        m_next = jnp.maximum(m_prev, jnp.max(s, axis=1, keepdims=True))
        n = s.shape[1] // LANES
        p = jnp.exp2(s - jnp.concatenate([m_next] * n, axis=1))
        alpha = jnp.exp2(m_prev - m_next)
        l_ref[g] = alpha * l_ref[g] + jnp.sum(p, axis=1, keepdims=True)
        pv = _dot(p[:, :page].astype(BF16), v_of(0))
        for j in range(1, s.shape[1] // page):
            pv = pv + _dot(p[:, j * page:(j + 1) * page].astype(BF16), v_of(j))
        acc_ref[g] = acc_ref[g] * alpha + pv
        m_ref[g] = m_next

    bias4 = jnp.concatenate([bias_ref[...]] * HEADS_PER_KV, axis=0)
    for g in range(KV_HEADS):
        qg = q_group(g)
        s = jnp.concatenate([_dot_nt(qg, head_rows(k_refs[j], g)) for j in range(pages_per_step)], axis=1)
        update(g, s + bias4, lambda j: head_rows(v_refs[j], g))

    @pl.when(step == pl.num_programs(1) - 1)
    def _():
        biasn4 = jnp.concatenate([biasn_ref[...]] * HEADS_PER_KV, axis=0)
        for g in range(KV_HEADS):
            hs = slice(g * HEAD_DIM, (g + 1) * HEAD_DIM)
            s = _dot_nt(q_group(g), kn_ref[:, hs]) + biasn4
            update(g, s, lambda j: vn_ref[:, hs])
            o = acc_ref[g] / l_ref[g]
            for hh in range(HEADS_PER_KV):
                sl = slice((g * HEADS_PER_KV + hh) * HEAD_DIM, (g * HEADS_PER_KV + hh + 1) * HEAD_DIM)
                o_ref[:, sl] = o[hh * t_new:(hh + 1) * t_new] * _silu(ga_ref[:, sl])


def _sample_attention(page_table, q, bias, bias_new, knew_pad, vnew_pad, z_rest, pool_k, pool_v, layer,
                      dec_batch, t_new, pages_per_step):
    n_pages = page_table.shape[1]
    page_rows = pool_k.shape[2]
    page = page_rows // KV_HEADS
    steps = n_pages // pages_per_step
    rows = HEADS_PER_KV * t_new
    per_b = lambda b, s, pt: (b, 0)

    def page_map(j):
        return lambda b, s, pt: (layer, pt[b, s * pages_per_step + j], 0, 0)

    page_specs = [pl.BlockSpec((None, None, page_rows, HEAD_DIM), page_map(j)) for j in range(pages_per_step)]
    grid_spec = pltpu.PrefetchScalarGridSpec(
        num_scalar_prefetch=1,
        grid=(dec_batch, steps),
        in_specs=[pl.BlockSpec((t_new, ATTN_W), per_b),
                  pl.BlockSpec((t_new, pages_per_step * page), lambda b, s, pt: (b, s)),
                  pl.BlockSpec((t_new, LANES), per_b),
                  pl.BlockSpec((None, LANES, KV_W), lambda b, s, pt: (b, 0, 0)),
                  pl.BlockSpec((None, LANES, KV_W), lambda b, s, pt: (b, 0, 0)),
                  pl.BlockSpec((t_new, ATTN_W), per_b)] + page_specs + page_specs,
        out_specs=pl.BlockSpec((t_new, ATTN_W), per_b),
        scratch_shapes=[pltpu.VMEM((KV_HEADS, rows, LANES), F32), pltpu.VMEM((KV_HEADS, rows, LANES), F32),
                        pltpu.VMEM((KV_HEADS, rows, HEAD_DIM), F32)],
    )
    return pl.pallas_call(
        functools.partial(_sample_attn_kernel, pages_per_step=pages_per_step, t_new=t_new),
        grid_spec=grid_spec,
        out_shape=jax.ShapeDtypeStruct((dec_batch * t_new, ATTN_W), F32),
        compiler_params=_cparams(2),
        name="sample_attention",
    )(page_table, q, bias, bias_new, knew_pad, vnew_pad, z_rest,
      *([pool_k] * pages_per_step), *([pool_v] * pages_per_step))


def _lru_kernel(xr_ref, gr_ref, cp_ref, hp_ref, cw_ref, cb_ref, wr_ref, br_ref, wg_ref, bg_ref, lam_ref,
                o_ref, hl_ref, cn_ref, ext_ref, hc_ref, a_ref, b_ref, *, tt, layer):
    t = pl.program_id(1)
    tail = CONV_W - 1
    base = SUBLANES
    width = xr_ref.shape[1]
    bw = width // LRU_BLOCKS
    vec = lambda ref: ref[layer:layer + 1, :]

    @pl.when(t == 0)
    def _():
        ext_ref[base - tail:base, :] = cp_ref[...]
        hc_ref[...] = hp_ref[...]

    x = xr_ref[...]
    ext_ref[base:base + tt, :] = x
    cw = cw_ref[...]
    xc = vec(cb_ref) + x * cw[tail:tail + 1, :]
    for j in range(tail):
        xc = xc + ext_ref[base - tail + j: base - tail + j + tt, :] * cw[j:j + 1, :]
    new_tail = x[tt - tail:, :]
    ext_ref[base - tail:base, :] = new_tail

    xcb = xc.astype(BF16)
    r_lin = jnp.concatenate([_dot(xcb[:, n * bw:(n + 1) * bw], wr_ref[n].astype(BF16))
                             for n in range(LRU_BLOCKS)], axis=1)
    g_lin = jnp.concatenate([_dot(xcb[:, n * bw:(n + 1) * bw], wg_ref[n].astype(BF16))
                             for n in range(LRU_BLOCKS)], axis=1)
    r = _sigmoid(r_lin + vec(br_ref))
    gi = _sigmoid(g_lin + vec(bg_ref))
    nl = -vec(lam_ref)
    softplus = jnp.maximum(nl, 0.0) + jnp.log1p(jnp.exp(-jnp.abs(nl)))
    log_a = (-LRU_C) * r * softplus
    a = jnp.exp(log_a)
    b = jnp.sqrt(jnp.tanh(-log_a) * (a * a + 1.0)) * (gi * xc)

    row = lax.broadcasted_iota(I32, (tt, width), 0) & (SUBLANES - 1)
    d = 1
    while d < SUBLANES:
        keep = row >= d
        a_s = jnp.where(keep, pltpu.roll(a, d, 0), 1.0)
        b_s = jnp.where(keep, pltpu.roll(b, d, 0), 0.0)
        b = a * b_s + b
        a = a * a_s
        d *= 2
    a_ref[...] = a
    b_ref[...] = b

    def group_body(gidx, h):
        off = pl.multiple_of(gidx * SUBLANES, SUBLANES)
        h8 = a_ref[pl.ds(off, SUBLANES), :] * h + b_ref[pl.ds(off, SUBLANES), :]
        b_ref[pl.ds(off, SUBLANES), :] = h8
        return h8[SUBLANES - 1:SUBLANES, :]

    h_last = lax.fori_loop(0, tt // SUBLANES, group_body, hc_ref[...])
    hc_ref[...] = h_last
    o_ref[...] = (b_ref[...] * _silu(gr_ref[...])).astype(o_ref.dtype)

    @pl.when(t == pl.num_programs(1) - 1)
    def _():
        hl_ref[...] = h_last
        cn_ref[...] = new_tail


def _lru(z_rest, conv_prev, h_prev, conv_w, conv_b, w_r, b_r, w_g, b_g, lam, layer, batch, t_len, tt, out_dtype):
    width = conv_w.shape[2]
    nt = t_len // tt
    tail = CONV_W - 1
    bw = width // LRU_BLOCKS
    whole = lambda a: pl.BlockSpec(a.shape, lambda b, t: (0,) * a.ndim)
    per_b = lambda b, t: (b, 0, 0)
    return pl.pallas_call(
        functools.partial(_lru_kernel, tt=tt, layer=layer),
        grid=(batch, nt),
        in_specs=[pl.BlockSpec((tt, width), lambda b, t: (b * nt + t, 1)),
                  pl.BlockSpec((tt, width), lambda b, t: (b * nt + t, 2)),
                  pl.BlockSpec((None, tail, width), per_b), pl.BlockSpec((None, 1, width), per_b),
                  pl.BlockSpec((None, CONV_W, width), lambda b, t: (layer, 0, 0)), whole(conv_b),
                  pl.BlockSpec((None, LRU_BLOCKS, bw, bw), lambda b, t: (layer, 0, 0, 0)), whole(b_r),
                  pl.BlockSpec((None, LRU_BLOCKS, bw, bw), lambda b, t: (layer, 0, 0, 0)), whole(b_g),
                  whole(lam)],
        out_specs=[pl.BlockSpec((tt, width), lambda b, t: (b * nt + t, 0)),
                   pl.BlockSpec((None, 1, width), per_b), pl.BlockSpec((None, tail, width), per_b)],
        out_shape=[jax.ShapeDtypeStruct((batch * t_len, width), out_dtype),
                   jax.ShapeDtypeStruct((batch, 1, width), F32),
                   jax.ShapeDtypeStruct((batch, tail, width), F32)],
        scratch_shapes=[pltpu.VMEM((SUBLANES + tt, width), F32), pltpu.VMEM((1, width), F32),
                        pltpu.VMEM((tt, width), F32), pltpu.VMEM((tt, width), F32)],
        compiler_params=_cparams(2),
        name="rglru",
    )(z_rest, z_rest, conv_prev, h_prev, conv_w, conv_b, w_r, b_r, w_g, b_g, lam)


def _mem_attn_kernel(qm_ref, gm_ref, mk_ref, mv_ref, o_ref):
    width = qm_ref.shape[1]
    hd = width // MEM_HEADS
    scale = hd ** -0.5
    for h in range(MEM_HEADS):
        sl = slice(h * hd, (h + 1) * hd)
        s = _dot_nt(qm_ref[:, sl].astype(BF16), mk_ref[:, sl].astype(BF16)) * scale
        p = jnp.exp(s - jnp.max(s, axis=1, keepdims=True))
        l = jnp.sum(p, axis=1, keepdims=True)
        o = _dot(p.astype(BF16), mv_ref[:, sl].astype(BF16)) / l
        o_ref[:, sl] = (o * _silu(gm_ref[:, sl])).astype(o_ref.dtype)


def _mem_attention(z_rest, mk, mv, batch, t_len, tm, out_dtype):
    width = mk.shape[2]
    mem = mk.shape[1]
    nt = t_len // tm
    return pl.pallas_call(
        _mem_attn_kernel,
        grid=(batch, nt),
        in_specs=[pl.BlockSpec((tm, width), lambda b, t: (b * nt + t, 3)),
                  pl.BlockSpec((tm, width), lambda b, t: (b * nt + t, 4)),
                  pl.BlockSpec((None, mem, width), lambda b, t: (b, 0, 0)),
                  pl.BlockSpec((None, mem, width), lambda b, t: (b, 0, 0))],
        out_specs=pl.BlockSpec((tm, width), lambda b, t: (b * nt + t, 0)),
        out_shape=jax.ShapeDtypeStruct((batch * t_len, width), out_dtype),
        compiler_params=_cparams(2),
        name="mem_attention",
    )(z_rest, z_rest, mk, mv)


def _merge_kernel(ba_ref, bl_ref, bm_ref, ga_ref, gl_ref, gm_ref, wa_ref, wl_ref, wm_ref, o_ref, wbf_ref):
    @pl.when(pl.program_id(1) == 0)
    def _():
        for b, w_ref in enumerate((wa_ref, wl_ref, wm_ref)):
            _cast_weight(wbf_ref.at[b], w_ref)

    out = _sigmoid(ga_ref[...]) * _dot(ba_ref[...].astype(BF16), wbf_ref[0])
    out = out + _sigmoid(gl_ref[...]) * _dot(bl_ref[...].astype(BF16), wbf_ref[1])
    out = out + _sigmoid(gm_ref[...]) * _dot(bm_ref[...].astype(BF16), wbf_ref[2])
    o_ref[...] = out.astype(BF16)


def _merge(branches, z_rest, w_branch, layer, tm, tn):
    m, bw = branches[0].shape
    d = w_branch.shape[3]
    gate_col0 = 5 * bw // tn
    br_spec = pl.BlockSpec((tm, bw), lambda j, i: (i, 0))

    def gate_spec(b):
        return pl.BlockSpec((tm, tn), lambda j, i: (i, gate_col0 + b * (d // tn) + j))

    def w_spec(b):
        return pl.BlockSpec((None, None, bw, tn), lambda j, i: (layer, b, 0, j))

    return pl.pallas_call(
        _merge_kernel,
        grid=(d // tn, m // tm),
        in_specs=[br_spec] * N_BRANCH + [gate_spec(b) for b in range(N_BRANCH)]
                 + [w_spec(b) for b in range(N_BRANCH)],
        out_specs=pl.BlockSpec((tm, tn), lambda j, i: (i, j)),
        out_shape=jax.ShapeDtypeStruct((m, d), BF16),
        scratch_shapes=[pltpu.VMEM((N_BRANCH, bw, tn), BF16)],
        compiler_params=_cparams(2),
        name="merge",
    )(*branches, z_rest, z_rest, z_rest, w_branch, w_branch, w_branch)


def _rope_tables(pos, reps):
    def tab(dim):
        half = dim // 2
        freq = ROPE_THETA ** (-jnp.arange(half, dtype=F32) / half)
        ang = pos.astype(F32)[:, None] * freq[None, :]
        cos, sin = jnp.cos(ang), jnp.sin(ang)
        n = LANES // dim
        c = jnp.tile(jnp.concatenate([cos, cos], axis=1), (reps, n))
        s = jnp.tile(jnp.concatenate([-sin, sin], axis=1), (reps, n))
        return c, s
    cq, sq = tab(HEAD_DIM)
    ci, si = tab(IDX_DIM)
    return cq, sq, ci, si


def _pick_tile(m, pref):
    t = min(m, pref)
    while m % t:
        t //= 2
    return t


def kernel(x_prompt, x_sample, mem_prompt, cache_k, cache_v, cache_idx_k, cache_mem_k, cache_mem_v, state_conv, state_h, page_table, norm_g, w_in, conv_w, conv_b, w_rgate, b_rgate, w_igate, b_igate, lru_lambda, mem_norm_g, w_mem_kv, w_branch, w_out, final_norm_g):
    bp, seq, d = x_prompt.shape
    bs, t_new, _ = x_sample.shape
    depth = w_in.shape[0]
    n_pool, page = cache_k.shape[1], cache_k.shape[2]
    n_pages = page_table.shape[1]
    past = n_pages * page
    mem_tokens = mem_prompt.shape[1]
    lru_w = conv_w.shape[2]
    mem_w = w_mem_kv.shape[2] // 2
    mem_hd = mem_w // MEM_HEADS
    rest0 = ATTN_W + 2 * KV_W + QI_W + IDX_DIM + IDX_HEADS
    n_rest = w_in.shape[2] - rest0
    topk_s = min(INDEX_TOPK, (past + t_new) // 4)
    pages_per_step = _pick_tile(n_pages, 16)

    tabs_p = _rope_tables(jnp.arange(seq), bp)
    tabs_s = _rope_tables(past + jnp.arange(t_new), bs)

    mp, ms = bp * seq, bs * t_new
    xp = x_prompt.reshape(mp, d)
    xs = x_sample.reshape(ms, d)
    mem2d = mem_prompt.reshape(bp * mem_tokens, d)
    conv0 = jnp.zeros((bp, CONV_W - 1, lru_w), F32)
    h0 = jnp.zeros((bp, 1, lru_w), F32)
    pool_k = cache_k.reshape(depth, n_pool, page * KV_HEADS, HEAD_DIM)
    pool_v = cache_v.reshape(depth, n_pool, page * KV_HEADS, HEAD_DIM)
    w_in_t = jnp.swapaxes(w_in, 1, 2)
    pool_ik_t = jnp.swapaxes(cache_idx_k, 2, 3)
    mem_k_s = cache_mem_k.reshape(depth, bs, mem_tokens, mem_w)
    mem_v_s = cache_mem_v.reshape(depth, bs, mem_tokens, mem_w)
    h_s0 = state_h.reshape(depth, bs, 1, lru_w)
    lru_w_args = (conv_w, conv_b, w_rgate, b_rgate, w_igate, b_igate, lru_lambda)

    tm_p = _pick_tile(mp, 256)
    tm_mm = _pick_tile(mp, 1024)
    tm_out = _pick_tile(mp, 512)

    st_p, st_s, mk_list, mv_list = [], [], [], []
    for l in range(depth):
        m_kv = _norm_matmul(mem2d, mem_norm_g, w_mem_kv, l, _pick_tile(bp * mem_tokens, 512), 1024)
        mk = m_kv[:, :mem_w].reshape(bp, mem_tokens, mem_w)
        mv = m_kv[:, mem_w:].reshape(bp, mem_tokens, mem_w)
        mk_list.append(mk.reshape(bp, mem_tokens, MEM_HEADS, mem_hd))
        mv_list.append(mv.reshape(bp, mem_tokens, MEM_HEADS, mem_hd))

        xn, k, v, ki, kb, kib, q_t, qi_t, wi_t, vb_t = _proj_attn(xp, norm_g, w_in_t, tabs_p, l, tm_p, True)
        z_rest = _matmul(xn, w_in_t, l, rest0, n_rest, tm_mm, 1024, True)
        br_att = _prompt_attention(qi_t, wi_t, kib, q_t, kb, vb_t, z_rest, bp, seq)
        br_lru, h_last, conv_new = _lru(z_rest, conv0, h0, *lru_w_args, l, bp, seq, _pick_tile(seq, 256), BF16)
        br_mem = _mem_attention(z_rest, mk, mv, bp, seq, _pick_tile(seq, 512), BF16)
        merged = _merge([br_att, br_lru, br_mem], z_rest, w_branch, l, tm_out, 512)
        xp = _matmul(merged, w_out, l, 0, d, tm_out, 1024, False, residual=xp)
        st_p.append((k.reshape(bp, seq, KV_HEADS, HEAD_DIM), v.reshape(bp, seq, KV_HEADS, HEAD_DIM),
                     ki.reshape(bp, seq, IDX_DIM), conv_new, h_last.reshape(bp, lru_w)))

        xn, k, v, ki, kb, kib, q, qi, wi, vb = _proj_attn(xs, norm_g, w_in_t, tabs_s, l, ms, False)
        z_rest = _matmul(xn, w_in_t, l, rest0, n_rest, ms, 1024, True)
        qs = qi.reshape(ms * IDX_HEADS, IDX_DIM)
        ws = wi.reshape(ms * IDX_HEADS, 1)
        pad_rows = lambda a: jnp.pad(a.reshape(bs, t_new, a.shape[1]), ((0, 0), (0, LANES - t_new), (0, 0)))
        sc = _sample_index(page_table, qs, ws, pool_ik_t, l, bs, t_new, _pick_tile(n_pages, 2 * pages_per_step))
        bias, bias_new = _sample_select(sc, qs, ws, pad_rows(kib), bs, t_new, topk_s)
        br_att = _sample_attention(page_table, q, bias, bias_new, pad_rows(kb), pad_rows(vb), z_rest,
                                   pool_k, pool_v, l, bs, t_new, pages_per_step)
        br_lru, h_last, conv_new = _lru(z_rest, state_conv[l], h_s0[l], *lru_w_args, l, bs, t_new, t_new, F32)
        br_mem = _mem_attention(z_rest, mem_k_s[l], mem_v_s[l], bs, t_new, t_new, F32)
        merged = _merge([br_att, br_lru, br_mem], z_rest, w_branch, l, ms, 512)
        xs = _matmul(merged, w_out, l, 0, d, ms, 1024, False, residual=xs)
        st_s.append((k.reshape(bs, t_new, KV_HEADS, HEAD_DIM), v.reshape(bs, t_new, KV_HEADS, HEAD_DIM),
                     ki.reshape(bs, t_new, IDX_DIM), conv_new, h_last.reshape(bs, lru_w)))

    g_f = final_norm_g.reshape(1, d)
    y_prompt = _rmsnorm(xp, g_f, _pick_tile(mp, 512)).reshape(bp, seq, d)
    y_sample = _rmsnorm(xs, g_f, ms).reshape(bs, t_new, d)
    k_p, v_p, ik_p, conv_p, h_p = [jnp.stack(t) for t in zip(*st_p)]
    k_s, v_s, ik_s, conv_s, h_s = [jnp.stack(t) for t in zip(*st_s)]
    return (y_prompt, y_sample, k_p, v_p, ik_p, conv_p, h_p, jnp.stack(mk_list), jnp.stack(mv_list),
            k_s, v_s, ik_s, conv_s, h_s)
```

```python
import functools

import jax
import jax.numpy as jnp
from jax import lax
from jax.experimental import pallas as pl
from jax.experimental.pallas import tpu as pltpu

F32 = jnp.float32
BF16 = jnp.bfloat16
I32 = jnp.int32

N_HEADS = 8
HEAD_DIM = 128
KV_HEADS = 2
HEADS_PER_KV = N_HEADS // KV_HEADS
IDX_HEADS = 8
IDX_DIM = 64
INDEX_TOPK = 256
LRU_BLOCKS = 8
CONV_W = 4
LRU_C = 8.0
MEM_HEADS = 4
N_BRANCH = 3
ROPE_THETA = 10000.0
NORM_EPS = 1e-6

ATTN_W = N_HEADS * HEAD_DIM
KV_W = KV_HEADS * HEAD_DIM
QI_W = IDX_HEADS * IDX_DIM

LANES = 128
SUBLANES = 8
VMEM_LIMIT = 52 * 1024 * 1024
CAST_ROWS = 256
COUNT_ROWS = 8 * SUBLANES
LOG2E = 1.4426950408889634
QK_SCALE = HEAD_DIM ** -0.5 * LOG2E

NEG_BIG = -1e30
INT_MIN = -(2 ** 31)
I16 = jnp.int16
HALF_BITS = 16
HALF_MASK = 2 ** HALF_BITS - 1
HALF_MIN = -(2 ** (HALF_BITS - 1))


def _cparams(n_axes):
    return pltpu.CompilerParams(dimension_semantics=("arbitrary",) * n_axes,
                                vmem_limit_bytes=VMEM_LIMIT)


def _sigmoid(x):
    return 1.0 / (1.0 + jnp.exp(-x))


def _silu(x):
    return x * _sigmoid(x)


def _dot(a, b):
    return jnp.dot(a, b, preferred_element_type=F32)


def _dot_nt(a, b):
    return lax.dot_general(a, b, (((1,), (1,)), ((), ())), preferred_element_type=F32)


def _rms(x, g):
    ms = jnp.mean(x * x, axis=-1, keepdims=True)
    return (x * lax.rsqrt(ms + NORM_EPS)) * g


def _cast_weight(dst_ref, src_ref):
    k = dst_ref.shape[0]
    for r in range(0, k, CAST_ROWS):
        rows = slice(r, min(r + CAST_ROWS, k))
        dst_ref[rows, :] = src_ref[rows, :].astype(BF16)


def _proj_attn_kernel(x_ref, g_ref, w_ref, cq_ref, sq_ref, ci_ref, si_ref,
                      xn_ref, k_ref, v_ref, ki_ref, kb_ref, kib_ref, q_ref, qi_ref, wi_ref, vb_ref,
                      wbf_ref, *, layer, transposed):
    @pl.when(pl.program_id(0) == 0)
    def _():
        _cast_weight(wbf_ref, w_ref)

    xnb = _rms(x_ref[...], g_ref[layer:layer + 1, :]).astype(BF16)
    xn_ref[...] = xnb
    z = _dot_nt(xnb, wbf_ref[...])

    cq, sq, ci, si = cq_ref[...], sq_ref[...], ci_ref[...], si_ref[...]
    lane = lax.broadcasted_iota(I32, cq.shape, 1)
    first_half = (lane & (IDX_DIM // 2)) == 0

    def rope128(t):
        return t * cq + pltpu.roll(t, HEAD_DIM // 2, 1) * sq

    def rope64(t):
        rot = jnp.where(first_half, pltpu.roll(t, LANES - IDX_DIM // 2, 1),
                        pltpu.roll(t, IDX_DIM // 2, 1))
        return t * ci + rot * si

    def tile(c):
        return z[:, c * LANES:(c + 1) * LANES]

    for h in range(N_HEADS):
        qh = rope128(tile(h)) * QK_SCALE
        if transposed:
            q_ref[h * HEAD_DIM:(h + 1) * HEAD_DIM, :] = qh.T.astype(BF16)
        else:
            q_ref[:, h * HEAD_DIM:(h + 1) * HEAD_DIM] = qh
    c0 = ATTN_W // LANES
    for h in range(KV_HEADS):
        sl = slice(h * HEAD_DIM, (h + 1) * HEAD_DIM)
        kr = rope128(tile(c0 + h))
        k_ref[:, sl] = kr
        kb_ref[:, sl] = kr.astype(BF16)
    c0 += KV_W // LANES
    for h in range(KV_HEADS):
        sl = slice(h * HEAD_DIM, (h + 1) * HEAD_DIM)
        vh = tile(c0 + h)
        v_ref[:, sl] = vh
        if transposed:
            vb_ref[sl, :] = vh.T.astype(BF16)
        else:
            vb_ref[:, sl] = vh.astype(BF16)
    c0 += KV_W // LANES
    for c in range(QI_W // LANES):
        qc = rope64(tile(c0 + c))
        if transposed:
            qi_ref[c * LANES:(c + 1) * LANES, :] = qc.T.astype(BF16)
        else:
            qi_ref[:, c * LANES:(c + 1) * LANES] = qc.astype(BF16)
    c0 += QI_W // LANES
    last = tile(c0)
    kir = rope64(last)[:, :IDX_DIM]
    ki_ref[...] = kir
    kib_ref[...] = kir.astype(BF16)
    if transposed:
        wi_ref[...] = last.T[IDX_DIM:IDX_DIM + IDX_HEADS, :] * (IDX_HEADS ** -0.5)
    else:
        wi_ref[...] = last[:, IDX_DIM:IDX_DIM + IDX_HEADS] * (IDX_HEADS ** -0.5)


def _proj_attn(x, norm_g, w_in_t, tabs, layer, tm, transposed):
    m, d = x.shape
    na = ATTN_W + 2 * KV_W + QI_W + LANES
    row = lambda i: (i, 0)
    col = lambda i: (0, i)
    const = lambda i: (0, 0)

    def tok(width, dtype):
        return ((m, width), dtype, pl.BlockSpec((tm, width), row))

    def feat(width, dtype):
        return ((width, m), dtype, pl.BlockSpec((width, tm), col))

    outs = [tok(d, BF16), tok(KV_W, F32), tok(KV_W, F32), tok(IDX_DIM, F32), tok(KV_W, BF16), tok(IDX_DIM, BF16)]
    if transposed:
        outs += [feat(ATTN_W, BF16), feat(QI_W, BF16), feat(IDX_HEADS, F32), feat(KV_W, BF16)]
    else:
        outs += [tok(ATTN_W, F32), tok(QI_W, BF16), tok(IDX_HEADS, F32), tok(KV_W, BF16)]
    return pl.pallas_call(
        functools.partial(_proj_attn_kernel, layer=layer, transposed=transposed),
        grid=(m // tm,),
        in_specs=[pl.BlockSpec((tm, d), row), pl.BlockSpec(norm_g.shape, const),
                  pl.BlockSpec((None, na, d), lambda i: (layer, 0, 0), pipeline_mode=pl.Buffered(1))]
                 + [pl.BlockSpec((tm, LANES), row)] * 4,
        out_specs=[o[2] for o in outs],
        out_shape=[jax.ShapeDtypeStruct(o[0], o[1]) for o in outs],
        scratch_shapes=[pltpu.VMEM((na, d), BF16)],
        compiler_params=_cparams(1),
        name="proj_attn",
    )(x, norm_g, w_in_t, *tabs)


def _mm_kernel(*refs, w_transposed, residual):
    x_ref, w_ref = refs[0], refs[1]
    r_ref = refs[2] if residual else None
    o_ref, wbf_ref = refs[-2], refs[-1]

    @pl.when(pl.program_id(1) == 0)
    def _():
        _cast_weight(wbf_ref, w_ref.at[0] if w_transposed else w_ref)

    out = _dot_nt(x_ref[...], wbf_ref[...]) if w_transposed else _dot(x_ref[...], wbf_ref[...])
    if residual:
        out = r_ref[...] + out
    o_ref[...] = out


def _matmul(x, w, layer, col0, n, tm, tn, w_transposed, residual=None):
    m, k = x.shape
    assert n % tn == 0 and m % tm == 0
    if w_transposed:
        assert col0 % SUBLANES == 0
        w_spec = pl.BlockSpec((pl.Element(1), pl.Element(tn), pl.Element(k)),
                              lambda j, i: (layer, (col0 // SUBLANES + j * (tn // SUBLANES)) * SUBLANES, 0))
        w_scratch = pltpu.VMEM((tn, k), BF16)
    else:
        assert col0 % tn == 0
        w_spec = pl.BlockSpec((None, k, tn), lambda j, i: (layer, 0, col0 // tn + j))
        w_scratch = pltpu.VMEM((k, tn), BF16)
    in_specs = [pl.BlockSpec((tm, k), lambda j, i: (i, 0)), w_spec]
    args = [x, w]
    if residual is not None:
        in_specs.append(pl.BlockSpec((tm, tn), lambda j, i: (i, j)))
        args.append(residual)
    return pl.pallas_call(
        functools.partial(_mm_kernel, w_transposed=w_transposed, residual=residual is not None),
        grid=(n // tn, m // tm),
        in_specs=in_specs,
        out_specs=pl.BlockSpec((tm, tn), lambda j, i: (i, j)),
        out_shape=jax.ShapeDtypeStruct((m, n), F32),
        scratch_shapes=[w_scratch],
        compiler_params=_cparams(2),
        name="matmul_residual" if residual is not None else "matmul",
    )(*args)


def _norm_mm_kernel(x_ref, g_ref, w_ref, o_ref, *, layer):
    xn = _rms(x_ref[...], g_ref[layer:layer + 1, :]).astype(BF16)
    o_ref[...] = _dot(xn, w_ref[...].astype(BF16))


def _norm_matmul(x, g, w, layer, tm, tn):
    m, k = x.shape
    n = w.shape[2]
    return pl.pallas_call(
        functools.partial(_norm_mm_kernel, layer=layer),
        grid=(n // tn, m // tm),
        in_specs=[pl.BlockSpec((tm, k), lambda j, i: (i, 0)), pl.BlockSpec(g.shape, lambda j, i: (0, 0)),
                  pl.BlockSpec((None, k, tn), lambda j, i: (layer, 0, j))],
        out_specs=pl.BlockSpec((tm, tn), lambda j, i: (i, j)),
        out_shape=jax.ShapeDtypeStruct((m, n), F32),
        compiler_params=_cparams(2),
        name="norm_matmul",
    )(x, g, w)


def _rmsnorm_kernel(x_ref, g_ref, o_ref):
    o_ref[...] = _rms(x_ref[...], g_ref[...])


def _rmsnorm(x, g, tm):
    m, d = x.shape
    return pl.pallas_call(
        _rmsnorm_kernel,
        grid=(m // tm,),
        in_specs=[pl.BlockSpec((tm, d), lambda i: (i, 0)), pl.BlockSpec((1, d), lambda i: (0, 0))],
        out_specs=pl.BlockSpec((tm, d), lambda i: (i, 0)),
        out_shape=jax.ShapeDtypeStruct((m, d), F32),
        compiler_params=_cparams(1),
        name="final_rmsnorm",
    )(x, g)


def _sortable_key(score):
    bits = pltpu.bitcast(score, I32)
    return bits ^ ((bits >> 31) & jnp.int32(0x7FFFFFFF))


def _kth_largest_key(count_ge, shape, topk, bits=32):
    zero = jnp.zeros(shape, I32)
    ans = jnp.where(count_ge(zero) >= topk, zero, jnp.full(shape, -(2 ** (bits - 1)), I32))

    def bit_body(j, ans):
        cand = ans | lax.shift_left(jnp.int32(1), bits - 2 - j)
        return jnp.where(count_ge(cand) >= topk, cand, ans)

    return lax.fori_loop(0, bits - 1, bit_body, ans)


def _tie_cutoff(count_tie_lt, need, shape, idx_bits):
    def bit_body(j, cut):
        cand = cut | lax.shift_left(jnp.int32(1), idx_bits - 1 - j)
        return jnp.where(count_tie_lt(cand) < need, cand, cut)

    return lax.fori_loop(0, idx_bits, bit_body, jnp.zeros(shape, I32))


def _prompt_attn_kernel(qi_ref, wi_ref, kib_ref, q_ref, kb_ref, vb_ref, ga_ref, o_ref,
                        key_ref, hi_ref, lo_ref, bias_ref, acc_ref, *, tq, kc, topk, seq):
    i = pl.program_id(1)
    nch = ((i + 1) * tq + kc - 1) // kc
    key_pos = lax.broadcasted_iota(I32, (kc, tq), 0)
    q_pos = i * tq + lax.broadcasted_iota(I32, (kc, tq), 1)
    idx_bits = max(1, (seq - 1).bit_length())
    row1 = (1, tq)

    def chunk(ref, c):
        return ref[pl.ds(pl.multiple_of(c * kc, kc), kc), :]

    wi = wi_ref[...] * (IDX_DIM ** -0.5)

    def score_body(c, carry):
        kic = chunk(kib_ref, c)
        acc = jnp.zeros((kc, tq), F32)
        for h in range(IDX_HEADS):
            s = _dot(kic, qi_ref[h * IDX_DIM:(h + 1) * IDX_DIM, :])
            acc = acc + jnp.maximum(s, 0.0) * wi[h:h + 1, :]
        key = jnp.where(c * kc + key_pos <= q_pos, _sortable_key(acc), INT_MIN)
        key_ref[pl.ds(pl.multiple_of(c * kc, kc), kc), :] = key
        hi_ref[pl.ds(pl.multiple_of(c * kc, kc), kc), :] = (key >> HALF_BITS).astype(I16)
        return carry

    lax.fori_loop(0, nch, score_body, 0)

    def count16_ge(ref, cand):
        c16 = cand.astype(I16)

        def body(c, acc):
            ind = jnp.where(chunk(ref, c) >= c16, jnp.ones((), I16), jnp.zeros((), I16))
            parts = [ind[r * COUNT_ROWS:(r + 1) * COUNT_ROWS] for r in range(kc // COUNT_ROWS)]
            while len(parts) > 1:
                parts = [a + b for a, b in zip(parts[::2], parts[1::2])]
            return acc + parts[0]

        acc = lax.fori_loop(0, nch, body, jnp.zeros((COUNT_ROWS, tq), I16))
        return jnp.sum(acc.astype(I32), axis=0, keepdims=True)

    def kth_largest():
        t_hi = _kth_largest_key(lambda cand: count16_ge(hi_ref, cand), row1, topk, bits=HALF_BITS)
        need_lo = topk - count16_ge(hi_ref, t_hi + 1)

        def low_body(c, carry):
            k = chunk(key_ref, c)
            lo = jnp.where((k >> HALF_BITS) == t_hi, (k & HALF_MASK) + HALF_MIN, HALF_MIN)
            lo_ref[pl.ds(pl.multiple_of(c * kc, kc), kc), :] = lo.astype(I16)
            return carry

        lax.fori_loop(0, nch, low_body, 0)
        t_lo = _kth_largest_key(lambda cand: count16_ge(lo_ref, cand), row1, need_lo, bits=HALF_BITS)
        thr = lax.shift_left(t_hi, HALF_BITS) | (t_lo - HALF_MIN)
        return jnp.where(t_hi == HALF_MIN, INT_MIN + 1, jnp.maximum(thr, INT_MIN + 1))

    def chunk_count(pred):
        def body(c, acc):
            ind = jnp.where(pred(chunk(key_ref, c), c * kc + key_pos), 1.0, 0.0)
            return acc + jnp.sum(ind.reshape(kc // COUNT_ROWS, COUNT_ROWS, tq), axis=0)
        acc = lax.fori_loop(0, nch, body, jnp.zeros((COUNT_ROWS, tq), F32))
        return jnp.sum(acc, axis=0, keepdims=True)

    def count_ge(cand):
        return chunk_count(lambda k, pos: k >= cand)

    def write_bias(keep):
        def body(c, carry):
            k = chunk(key_ref, c)
            bias_ref[pl.ds(pl.multiple_of(c * kc, kc), kc), :] = jnp.where(keep(k, c * kc + key_pos), 0.0, NEG_BIG)
            return carry
        lax.fori_loop(0, nch, body, 0)

    need_search = (i + 1) * tq > topk
    thr = lax.cond(need_search, kth_largest, lambda: jnp.full(row1, INT_MIN + 1, I32))
    write_bias(lambda k, pos: k >= thr)

    @pl.when(jnp.max(count_ge(thr)) > topk)
    def _():
        need = topk - count_ge(thr + 1)
        count_tie_lt = lambda cut: chunk_count(lambda k, pos: jnp.where(k == thr, pos, seq) < cut)
        cut = _tie_cutoff(count_tie_lt, need, row1, idx_bits)
        write_bias(lambda k, pos: (k > thr) | ((k == thr) & (pos <= cut)))

    qw = HEADS_PER_KV * tq
    q4 = [jnp.concatenate([q_ref[(g * HEADS_PER_KV + hh) * HEAD_DIM:(g * HEADS_PER_KV + hh + 1) * HEAD_DIM, :]
                           for hh in range(HEADS_PER_KV)], axis=1) for g in range(KV_HEADS)]
    acc_ref[...] = jnp.zeros(acc_ref.shape, F32)

    def attn_body(c, carry):
        off = pl.multiple_of(c * kc, kc)
        bias = chunk(bias_ref, c)
        bias4 = jnp.concatenate([bias] * HEADS_PER_KV, axis=1)
        out = []
        for g in range(KV_HEADS):
            m_prev, l_prev = carry[g]
            hs = slice(g * HEAD_DIM, (g + 1) * HEAD_DIM)
            s = _dot(kb_ref[pl.ds(off, kc), hs], q4[g]) + bias4
            m_next = jnp.maximum(m_prev, jnp.max(s, axis=0, keepdims=True))
            p = jnp.exp2(s - m_next)
            alpha = jnp.exp2(m_prev - m_next)
            l_next = alpha * l_prev + jnp.sum(p, axis=0, keepdims=True)
            acc_ref[g] = acc_ref[g] * alpha + _dot(vb_ref[hs, pl.ds(off, kc)], p.astype(BF16))
            out.append((m_next, l_next))
        return tuple(out)

    init = tuple((jnp.full((1, qw), NEG_BIG, F32), jnp.zeros((1, qw), F32)) for _ in range(KV_HEADS))
    stats = lax.fori_loop(0, nch, attn_body, init)
    for g in range(KV_HEADS):
        o_t = acc_ref[g] / stats[g][1]
        for hh in range(HEADS_PER_KV):
            sl = slice((g * HEADS_PER_KV + hh) * HEAD_DIM, (g * HEADS_PER_KV + hh + 1) * HEAD_DIM)
            o_ref[:, sl] = (o_t[:, hh * tq:(hh + 1) * tq].T * _silu(ga_ref[:, sl])).astype(BF16)


def _prompt_attention(qi_t, wi_t, kib, q_t, kb, vb_t, z_rest, batch, seq, tq=128):
    nq = seq // tq
    kc = _pick_tile(seq, 512)
    topk = min(INDEX_TOPK, seq // 4)
    blk_t = lambda b, i: (0, b * nq + i)
    blk = lambda b, i: (b * nq + i, 0)
    return pl.pallas_call(
        functools.partial(_prompt_attn_kernel, tq=tq, kc=kc, topk=topk, seq=seq),
        grid=(batch, nq),
        in_specs=[pl.BlockSpec((QI_W, tq), blk_t), pl.BlockSpec((IDX_HEADS, tq), blk_t),
                  pl.BlockSpec((seq, IDX_DIM), lambda b, i: (b, 0)), pl.BlockSpec((ATTN_W, tq), blk_t),
                  pl.BlockSpec((seq, KV_W), lambda b, i: (b, 0)), pl.BlockSpec((KV_W, seq), lambda b, i: (0, b)),
                  pl.BlockSpec((tq, ATTN_W), blk)],
        out_specs=pl.BlockSpec((tq, ATTN_W), blk),
        out_shape=jax.ShapeDtypeStruct((batch * seq, ATTN_W), BF16),
        scratch_shapes=[pltpu.VMEM((seq, tq), I32), pltpu.VMEM((seq, tq), I16), pltpu.VMEM((seq, tq), I16),
                        pltpu.VMEM((seq, tq), F32),
                        pltpu.VMEM((KV_HEADS, HEAD_DIM, HEADS_PER_KV * tq), F32)],
        compiler_params=_cparams(2),
        name="prompt_attention",
    )(qi_t, wi_t, kib, q_t, kb, vb_t, z_rest)


def _sample_index_kernel(pt_ref, qs_ref, ws_ref, *rest, pages_per_step, t_new):
    page_refs, o_ref = rest[:pages_per_step], rest[pages_per_step]
    qs = qs_ref[...]
    ws = ws_ref[...] * (IDX_DIM ** -0.5)
    page = page_refs[0].shape[1]
    wsb = jnp.broadcast_to(ws, (t_new * IDX_HEADS, page))
    for j in range(pages_per_step):
        kp = page_refs[j][...].astype(BF16)
        s = jnp.maximum(_dot(qs, kp), 0.0) * wsb
        o_ref[:, j * page:(j + 1) * page] = jnp.sum(s.reshape(t_new, IDX_HEADS, page), axis=1)


def _sample_index(page_table, qs, ws, pool_ik_t, layer, dec_batch, t_new, pages_per_step):
    n_pages = page_table.shape[1]
    page = pool_ik_t.shape[3]
    steps = n_pages // pages_per_step
    rows = t_new * IDX_HEADS

    def page_map(j):
        return lambda b, s, pt: (layer, pt[b, s * pages_per_step + j], 0, 0)

    grid_spec = pltpu.PrefetchScalarGridSpec(
        num_scalar_prefetch=1,
        grid=(dec_batch, steps),
        in_specs=[pl.BlockSpec((rows, IDX_DIM), lambda b, s, pt: (b, 0)),
                  pl.BlockSpec((rows, 1), lambda b, s, pt: (b, 0))]
                 + [pl.BlockSpec((None, None, IDX_DIM, page), page_map(j)) for j in range(pages_per_step)],
        out_specs=pl.BlockSpec((t_new, pages_per_step * page), lambda b, s, pt: (b, s)),
    )
    return pl.pallas_call(
        functools.partial(_sample_index_kernel, pages_per_step=pages_per_step, t_new=t_new),
        grid_spec=grid_spec,
        out_shape=jax.ShapeDtypeStruct((dec_batch * t_new, n_pages * page), F32),
        compiler_params=_cparams(2),
        name="sample_index",
    )(page_table, qs, ws, *([pool_ik_t] * pages_per_step))


def _sample_select_kernel(sc_ref, qs_ref, ws_ref, kn_ref, bias_ref, bias_new_ref, key_ref, keyn_ref,
                          *, dec_batch, t_new, past, topk):
    rows = dec_batch * t_new
    hrows = t_new * IDX_HEADS
    ws = ws_ref[...] * (IDX_DIM ** -0.5)
    lane_n = lax.broadcasted_iota(I32, (rows, LANES), 1)
    row_n = lax.broadcasted_iota(I32, (rows, LANES), 0) % t_new
    s_new = []
    for b in range(dec_batch):
        hs = slice(b * hrows, (b + 1) * hrows)
        s = jnp.maximum(_dot_nt(qs_ref[hs, :], kn_ref[b]), 0.0) * jnp.broadcast_to(ws[hs, :], (hrows, LANES))
        s_new.append(jnp.sum(s.reshape(t_new, IDX_HEADS, LANES), axis=1))
    s_new = jnp.concatenate(s_new, axis=0) + 0.0
    keyn_ref[...] = jnp.where(lane_n <= row_n, _sortable_key(s_new), INT_MIN)
    key_ref[...] = _sortable_key(sc_ref[...] + 0.0)
    pos = lax.broadcasted_iota(I32, (rows, past), 1)
    idx_bits = (past + LANES - 1).bit_length()
    col1 = (rows, 1)

    def count(pred):
        a = jnp.sum(jnp.where(pred(key_ref[...], pos), 1.0, 0.0), axis=1, keepdims=True)
        b = jnp.sum(jnp.where(pred(keyn_ref[...], past + lane_n), 1.0, 0.0), axis=1, keepdims=True)
        return a + b

    count_ge = lambda cand: count(lambda k, p: k >= cand)
    thr = jnp.maximum(_kth_largest_key(count_ge, col1, topk), INT_MIN + 1)
    bias_ref[...] = jnp.where(key_ref[...] >= thr, 0.0, NEG_BIG)
    bias_new_ref[...] = jnp.where(keyn_ref[...] >= thr, 0.0, NEG_BIG)

    @pl.when(jnp.max(count_ge(thr)) > topk)
    def _():
        need = topk - count_ge(thr + 1)
        big = jnp.int32(2 ** idx_bits)
        count_tie_lt = lambda cut: count(lambda k, p: jnp.where(k == thr, p, big) < cut)
        cut = _tie_cutoff(count_tie_lt, need, col1, idx_bits)
        keep = lambda k, p: (k > thr) | ((k == thr) & (p <= cut))
        bias_ref[...] = jnp.where(keep(key_ref[...], pos), 0.0, NEG_BIG)
        bias_new_ref[...] = jnp.where(keep(keyn_ref[...], past + lane_n), 0.0, NEG_BIG)


def _sample_select(sc, qs, ws, knew_pad, dec_batch, t_new, topk):
    rows, past = sc.shape
    whole = lambda a: pl.BlockSpec(a.shape, lambda i: (0,) * a.ndim)
    out_shape = [jax.ShapeDtypeStruct((rows, past), F32), jax.ShapeDtypeStruct((rows, LANES), F32)]
    return pl.pallas_call(
        functools.partial(_sample_select_kernel, dec_batch=dec_batch, t_new=t_new, past=past, topk=topk),
        grid=(1,),
        in_specs=[whole(sc), whole(qs), whole(ws), whole(knew_pad)],
        out_specs=[whole(o) for o in out_shape],
        out_shape=out_shape,
        scratch_shapes=[pltpu.VMEM((rows, past), I32), pltpu.VMEM((rows, LANES), I32)],
        compiler_params=_cparams(1),
        name="sample_select",
    )(sc, qs, ws, knew_pad)


def _sample_attn_kernel(pt_ref, q_ref, bias_ref, biasn_ref, kn_ref, vn_ref, ga_ref, *rest,
                        pages_per_step, t_new):
    k_refs = rest[:pages_per_step]
    v_refs = rest[pages_per_step:2 * pages_per_step]
    o_ref, m_ref, l_ref, acc_ref = rest[2 * pages_per_step:]
    step = pl.program_id(1)
    page = k_refs[0].shape[0] // KV_HEADS

    @pl.when(step == 0)
    def _():
        m_ref[...] = jnp.full(m_ref.shape, NEG_BIG, F32)
        l_ref[...] = jnp.zeros(l_ref.shape, F32)
        acc_ref[...] = jnp.zeros(acc_ref.shape, F32)

    def q_group(g):
        return jnp.concatenate(
            [q_ref[:, (g * HEADS_PER_KV + hh) * HEAD_DIM:(g * HEADS_PER_KV + hh + 1) * HEAD_DIM]
             for hh in range(HEADS_PER_KV)], axis=0).astype(BF16)

    def head_rows(ref, g):
        return ref[pl.ds(g, page, stride=KV_HEADS), :].astype(BF16)

    def update(g, s, v_of):
        m_prev = m_ref[g]
        m_next = jnp.maximum(m_prev, jnp.max(s, axis=1, keepdims=True))
        n = s.shape[1] // LANES
        p = jnp.exp2(s - jnp.concatenate([m_next] * n, axis=1))
        alpha = jnp.exp2(m_prev - m_next)
        l_ref[g] = alpha * l_ref[g] + jnp.sum(p, axis=1, keepdims=True)
        pv = _dot(p[:, :page].astype(BF16), v_of(0))
        for j in range(1, s.shape[1] // page):
            pv = pv + _dot(p[:, j * page:(j + 1) * page].astype(BF16), v_of(j))
        acc_ref[g] = acc_ref[g] * alpha + pv
        m_ref[g] = m_next

    bias4 = jnp.concatenate([bias_ref[...]] * HEADS_PER_KV, axis=0)
    for g in range(KV_HEADS):
        qg = q_group(g)
        s = jnp.concatenate([_dot_nt(qg, head_rows(k_refs[j], g)) for j in range(pages_per_step)], axis=1)
        update(g, s + bias4, lambda j: head_rows(v_refs[j], g))

    @pl.when(step == pl.num_programs(1) - 1)
    def _():
        biasn4 = jnp.concatenate([biasn_ref[...]] * HEADS_PER_KV, axis=0)
        for g in range(KV_HEADS):
            hs = slice(g * HEAD_DIM, (g + 1) * HEAD_DIM)
            s = _dot_nt(q_group(g), kn_ref[:, hs]) + biasn4
            update(g, s, lambda j: vn_ref[:, hs])
            o = acc_ref[g] / l_ref[g]
            for hh in range(HEADS_PER_KV):
                sl = slice((g * HEADS_PER_KV + hh) * HEAD_DIM, (g * HEADS_PER_KV + hh + 1) * HEAD_DIM)
                o_ref[:, sl] = o[hh * t_new:(hh + 1) * t_new] * _silu(ga_ref[:, sl])


def _sample_attention(page_table, q, bias, bias_new, knew_pad, vnew_pad, z_rest, pool_k, pool_v, layer,
                      dec_batch, t_new, pages_per_step):
    n_pages = page_table.shape[1]
    page_rows = pool_k.shape[2]
    page = page_rows // KV_HEADS
    steps = n_pages // pages_per_step
    rows = HEADS_PER_KV * t_new
    per_b = lambda b, s, pt: (b, 0)

    def page_map(j):
        return lambda b, s, pt: (layer, pt[b, s * pages_per_step + j], 0, 0)

    page_specs = [pl.BlockSpec((None, None, page_rows, HEAD_DIM), page_map(j)) for j in range(pages_per_step)]
    grid_spec = pltpu.PrefetchScalarGridSpec(
        num_scalar_prefetch=1,
        grid=(dec_batch, steps),
        in_specs=[pl.BlockSpec((t_new, ATTN_W), per_b),
                  pl.BlockSpec((t_new, pages_per_step * page), lambda b, s, pt: (b, s)),
                  pl.BlockSpec((t_new, LANES), per_b),
                  pl.BlockSpec((None, LANES, KV_W), lambda b, s, pt: (b, 0, 0)),
                  pl.BlockSpec((None, LANES, KV_W), lambda b, s, pt: (b, 0, 0)),
                  pl.BlockSpec((t_new, ATTN_W), per_b)] + page_specs + page_specs,
        out_specs=pl.BlockSpec((t_new, ATTN_W), per_b),
        scratch_shapes=[pltpu.VMEM((KV_HEADS, rows, LANES), F32), pltpu.VMEM((KV_HEADS, rows, LANES), F32),
                        pltpu.VMEM((KV_HEADS, rows, HEAD_DIM), F32)],
    )
    return pl.pallas_call(
        functools.partial(_sample_attn_kernel, pages_per_step=pages_per_step, t_new=t_new),
        grid_spec=grid_spec,
        out_shape=jax.ShapeDtypeStruct((dec_batch * t_new, ATTN_W), F32),
        compiler_params=_cparams(2),
        name="sample_attention",
    )(page_table, q, bias, bias_new, knew_pad, vnew_pad, z_rest,
      *([pool_k] * pages_per_step), *([pool_v] * pages_per_step))


def _lru_kernel(xr_ref, gr_ref, cp_ref, hp_ref, cw_ref, cb_ref, wr_ref, br_ref, wg_ref, bg_ref, lam_ref,
                o_ref, hl_ref, cn_ref, ext_ref, hc_ref, a_ref, b_ref, *, tt, layer):
    t = pl.program_id(1)
    tail = CONV_W - 1
    base = SUBLANES
    width = xr_ref.shape[1]
    bw = width // LRU_BLOCKS
    vec = lambda ref: ref[layer:layer + 1, :]

    @pl.when(t == 0)
    def _():
        ext_ref[base - tail:base, :] = cp_ref[...]
        hc_ref[...] = hp_ref[...]

    x = xr_ref[...]
    ext_ref[base:base + tt, :] = x
    cw = cw_ref[...]
    xc = vec(cb_ref) + x * cw[tail:tail + 1, :]
    for j in range(tail):
        xc = xc + ext_ref[base - tail + j: base - tail + j + tt, :] * cw[j:j + 1, :]
    new_tail = x[tt - tail:, :]
    ext_ref[base - tail:base, :] = new_tail

    xcb = xc.astype(BF16)
    r_lin = jnp.concatenate([_dot(xcb[:, n * bw:(n + 1) * bw], wr_ref[n].astype(BF16))
                             for n in range(LRU_BLOCKS)], axis=1)
    g_lin = jnp.concatenate([_dot(xcb[:, n * bw:(n + 1) * bw], wg_ref[n].astype(BF16))
                             for n in range(LRU_BLOCKS)], axis=1)
    r = _sigmoid(r_lin + vec(br_ref))
    gi = _sigmoid(g_lin + vec(bg_ref))
    nl = -vec(lam_ref)
    softplus = jnp.maximum(nl, 0.0) + jnp.log1p(jnp.exp(-jnp.abs(nl)))
    log_a = (-LRU_C) * r * softplus
    a = jnp.exp(log_a)
    b = jnp.sqrt(jnp.tanh(-log_a) * (a * a + 1.0)) * (gi * xc)

    row = lax.broadcasted_iota(I32, (tt, width), 0) & (SUBLANES - 1)
    d = 1
    while d < SUBLANES:
        keep = row >= d
        a_s = jnp.where(keep, pltpu.roll(a, d, 0), 1.0)
        b_s = jnp.where(keep, pltpu.roll(b, d, 0), 0.0)
        b = a * b_s + b
        a = a * a_s
        d *= 2
    a_ref[...] = a
    b_ref[...] = b

    def group_body(gidx, h):
        off = pl.multiple_of(gidx * SUBLANES, SUBLANES)
        h8 = a_ref[pl.ds(off, SUBLANES), :] * h + b_ref[pl.ds(off, SUBLANES), :]
        b_ref[pl.ds(off, SUBLANES), :] = h8
        return h8[SUBLANES - 1:SUBLANES, :]

    h_last = lax.fori_loop(0, tt // SUBLANES, group_body, hc_ref[...])
    hc_ref[...] = h_last
    o_ref[...] = (b_ref[...] * _silu(gr_ref[...])).astype(o_ref.dtype)

    @pl.when(t == pl.num_programs(1) - 1)
    def _():
        hl_ref[...] = h_last
        cn_ref[...] = new_tail


def _lru(z_rest, conv_prev, h_prev, conv_w, conv_b, w_r, b_r, w_g, b_g, lam, layer, batch, t_len, tt, out_dtype):
    width = conv_w.shape[2]
    nt = t_len // tt
    tail = CONV_W - 1
    bw = width // LRU_BLOCKS
    whole = lambda a: pl.BlockSpec(a.shape, lambda b, t: (0,) * a.ndim)
    per_b = lambda b, t: (b, 0, 0)
    return pl.pallas_call(
        functools.partial(_lru_kernel, tt=tt, layer=layer),
        grid=(batch, nt),
        in_specs=[pl.BlockSpec((tt, width), lambda b, t: (b * nt + t, 1)),
                  pl.BlockSpec((tt, width), lambda b, t: (b * nt + t, 2)),
                  pl.BlockSpec((None, tail, width), per_b), pl.BlockSpec((None, 1, width), per_b),
                  pl.BlockSpec((None, CONV_W, width), lambda b, t: (layer, 0, 0)), whole(conv_b),
                  pl.BlockSpec((None, LRU_BLOCKS, bw, bw), lambda b, t: (layer, 0, 0, 0)), whole(b_r),
                  pl.BlockSpec((None, LRU_BLOCKS, bw, bw), lambda b, t: (layer, 0, 0, 0)), whole(b_g),
                  whole(lam)],
        out_specs=[pl.BlockSpec((tt, width), lambda b, t: (b * nt + t, 0)),
                   pl.BlockSpec((None, 1, width), per_b), pl.BlockSpec((None, tail, width), per_b)],
        out_shape=[jax.ShapeDtypeStruct((batch * t_len, width), out_dtype),
                   jax.ShapeDtypeStruct((batch, 1, width), F32),
                   jax.ShapeDtypeStruct((batch, tail, width), F32)],
        scratch_shapes=[pltpu.VMEM((SUBLANES + tt, width), F32), pltpu.VMEM((1, width), F32),
                        pltpu.VMEM((tt, width), F32), pltpu.VMEM((tt, width), F32)],
        compiler_params=_cparams(2),
        name="rglru",
    )(z_rest, z_rest, conv_prev, h_prev, conv_w, conv_b, w_r, b_r, w_g, b_g, lam)


def _mem_attn_kernel(qm_ref, gm_ref, mk_ref, mv_ref, o_ref):
    width = qm_ref.shape[1]
    hd = width // MEM_HEADS
    scale = hd ** -0.5
    for h in range(MEM_HEADS):
        sl = slice(h * hd, (h + 1) * hd)
        s = _dot_nt(qm_ref[:, sl].astype(BF16), mk_ref[:, sl].astype(BF16)) * scale
        p = jnp.exp(s - jnp.max(s, axis=1, keepdims=True))
        l = jnp.sum(p, axis=1, keepdims=True)
        o = _dot(p.astype(BF16), mv_ref[:, sl].astype(BF16)) / l
        o_ref[:, sl] = (o * _silu(gm_ref[:, sl])).astype(o_ref.dtype)


def _mem_attention(z_rest, mk, mv, batch, t_len, tm, out_dtype):
    width = mk.shape[2]
    mem = mk.shape[1]
    nt = t_len // tm
    return pl.pallas_call(
        _mem_attn_kernel,
        grid=(batch, nt),
        in_specs=[pl.BlockSpec((tm, width), lambda b, t: (b * nt + t, 3)),
                  pl.BlockSpec((tm, width), lambda b, t: (b * nt + t, 4)),
                  pl.BlockSpec((None, mem, width), lambda b, t: (b, 0, 0)),
                  pl.BlockSpec((None, mem, width), lambda b, t: (b, 0, 0))],
        out_specs=pl.BlockSpec((tm, width), lambda b, t: (b * nt + t, 0)),
        out_shape=jax.ShapeDtypeStruct((batch * t_len, width), out_dtype),
        compiler_params=_cparams(2),
        name="mem_attention",
    )(z_rest, z_rest, mk, mv)


def _merge_kernel(ba_ref, bl_ref, bm_ref, ga_ref, gl_ref, gm_ref, wa_ref, wl_ref, wm_ref, o_ref, wbf_ref):
    @pl.when(pl.program_id(1) == 0)
    def _():
        for b, w_ref in enumerate((wa_ref, wl_ref, wm_ref)):
            _cast_weight(wbf_ref.at[b], w_ref)

    out = _sigmoid(ga_ref[...]) * _dot(ba_ref[...].astype(BF16), wbf_ref[0])
    out = out + _sigmoid(gl_ref[...]) * _dot(bl_ref[...].astype(BF16), wbf_ref[1])
    out = out + _sigmoid(gm_ref[...]) * _dot(bm_ref[...].astype(BF16), wbf_ref[2])
    o_ref[...] = out.astype(BF16)


def _merge(branches, z_rest, w_branch, layer, tm, tn):
    m, bw = branches[0].shape
    d = w_branch.shape[3]
    gate_col0 = 5 * bw // tn
    br_spec = pl.BlockSpec((tm, bw), lambda j, i: (i, 0))

    def gate_spec(b):
        return pl.BlockSpec((tm, tn), lambda j, i: (i, gate_col0 + b * (d // tn) + j))

    def w_spec(b):
        return pl.BlockSpec((None, None, bw, tn), lambda j, i: (layer, b, 0, j))

    return pl.pallas_call(
        _merge_kernel,
        grid=(d // tn, m // tm),
        in_specs=[br_spec] * N_BRANCH + [gate_spec(b) for b in range(N_BRANCH)]
                 + [w_spec(b) for b in range(N_BRANCH)],
        out_specs=pl.BlockSpec((tm, tn), lambda j, i: (i, j)),
        out_shape=jax.ShapeDtypeStruct((m, d), BF16),
        scratch_shapes=[pltpu.VMEM((N_BRANCH, bw, tn), BF16)],
        compiler_params=_cparams(2),
        name="merge",
    )(*branches, z_rest, z_rest, z_rest, w_branch, w_branch, w_branch)


def _rope_tables(pos, reps):
    def tab(dim):
        half = dim // 2
        freq = ROPE_THETA ** (-jnp.arange(half, dtype=F32) / half)
        ang = pos.astype(F32)[:, None] * freq[None, :]
        cos, sin = jnp.cos(ang), jnp.sin(ang)
        n = LANES // dim
        c = jnp.tile(jnp.concatenate([cos, cos], axis=1), (reps, n))
        s = jnp.tile(jnp.concatenate([-sin, sin], axis=1), (reps, n))
        return c, s
    cq, sq = tab(HEAD_DIM)
    ci, si = tab(IDX_DIM)
    return cq, sq, ci, si


def _pick_tile(m, pref):
    t = min(m, pref)
    while m % t:
        t //= 2
    return t


def kernel(x_prompt, x_sample, mem_prompt, cache_k, cache_v, cache_idx_k, cache_mem_k, cache_mem_v, state_conv, state_h, page_table, norm_g, w_in, conv_w, conv_b, w_rgate, b_rgate, w_igate, b_igate, lru_lambda, mem_norm_g, w_mem_kv, w_branch, w_out, final_norm_g):
    bp, seq, d = x_prompt.shape
    bs, t_new, _ = x_sample.shape
    depth = w_in.shape[0]
    n_pool, page = cache_k.shape[1], cache_k.shape[2]
    n_pages = page_table.shape[1]
    past = n_pages * page
    mem_tokens = mem_prompt.shape[1]
    lru_w = conv_w.shape[2]
    mem_w = w_mem_kv.shape[2] // 2
    mem_hd = mem_w // MEM_HEADS
    rest0 = ATTN_W + 2 * KV_W + QI_W + IDX_DIM + IDX_HEADS
    n_rest = w_in.shape[2] - rest0
    topk_s = min(INDEX_TOPK, (past + t_new) // 4)
    pages_per_step = _pick_tile(n_pages, 16)

    tabs_p = _rope_tables(jnp.arange(seq), bp)
    tabs_s = _rope_tables(past + jnp.arange(t_new), bs)

    mp, ms = bp * seq, bs * t_new
    xp = x_prompt.reshape(mp, d)
    xs = x_sample.reshape(ms, d)
    mem2d = mem_prompt.reshape(bp * mem_tokens, d)
    conv0 = jnp.zeros((bp, CONV_W - 1, lru_w), F32)
    h0 = jnp.zeros((bp, 1, lru_w), F32)
    pool_k = cache_k.reshape(depth, n_pool, page * KV_HEADS, HEAD_DIM)
    pool_v = cache_v.reshape(depth, n_pool, page * KV_HEADS, HEAD_DIM)
    w_in_t = jnp.swapaxes(w_in, 1, 2)
    pool_ik_t = jnp.swapaxes(cache_idx_k, 2, 3)
    mem_k_s = cache_mem_k.reshape(depth, bs, mem_tokens, mem_w)
    mem_v_s = cache_mem_v.reshape(depth, bs, mem_tokens, mem_w)
    h_s0 = state_h.reshape(depth, bs, 1, lru_w)
    lru_w_args = (conv_w, conv_b, w_rgate, b_rgate, w_igate, b_igate, lru_lambda)

    tm_p = _pick_tile(mp, 256)
    tm_mm = _pick_tile(mp, 1024)
    tm_out = _pick_tile(mp, 512)

    st_p, st_s, mk_list, mv_list = [], [], [], []
    for l in range(depth):
        m_kv = _norm_matmul(mem2d, mem_norm_g, w_mem_kv, l, _pick_tile(bp * mem_tokens, 512), 1024)
        mk = m_kv[:, :mem_w].reshape(bp, mem_tokens, mem_w)
        mv = m_kv[:, mem_w:].reshape(bp, mem_tokens, mem_w)
        mk_list.append(mk.reshape(bp, mem_tokens, MEM_HEADS, mem_hd))
        mv_list.append(mv.reshape(bp, mem_tokens, MEM_HEADS, mem_hd))

        xn, k, v, ki, kb, kib, q_t, qi_t, wi_t, vb_t = _proj_attn(xp, norm_g, w_in_t, tabs_p, l, tm_p, True)
        z_rest = _matmul(xn, w_in_t, l, rest0, n_rest, tm_mm, 1024, True)
        br_att = _prompt_attention(qi_t, wi_t, kib, q_t, kb, vb_t, z_rest, bp, seq)
        br_lru, h_last, conv_new = _lru(z_rest, conv0, h0, *lru_w_args, l, bp, seq, _pick_tile(seq, 256), BF16)
        br_mem = _mem_attention(z_rest, mk, mv, bp, seq, _pick_tile(seq, 512), BF16)
        merged = _merge([br_att, br_lru, br_mem], z_rest, w_branch, l, tm_out, 512)
        xp = _matmul(merged, w_out, l, 0, d, tm_out, 1024, False, residual=xp)
        st_p.append((k.reshape(bp, seq, KV_HEADS, HEAD_DIM), v.reshape(bp, seq, KV_HEADS, HEAD_DIM),
                     ki.reshape(bp, seq, IDX_DIM), conv_new, h_last.reshape(bp, lru_w)))

        xn, k, v, ki, kb, kib, q, qi, wi, vb = _proj_attn(xs, norm_g, w_in_t, tabs_s, l, ms, False)
        z_rest = _matmul(xn, w_in_t, l, rest0, n_rest, ms, 1024, True)
        qs = qi.reshape(ms * IDX_HEADS, IDX_DIM)
        ws = wi.reshape(ms * IDX_HEADS, 1)
        pad_rows = lambda a: jnp.pad(a.reshape(bs, t_new, a.shape[1]), ((0, 0), (0, LANES - t_new), (0, 0)))
        sc = _sample_index(page_table, qs, ws, pool_ik_t, l, bs, t_new, _pick_tile(n_pages, 2 * pages_per_step))
        bias, bias_new = _sample_select(sc, qs, ws, pad_rows(kib), bs, t_new, topk_s)
        br_att = _sample_attention(page_table, q, bias, bias_new, pad_rows(kb), pad_rows(vb), z_rest,
                                   pool_k, pool_v, l, bs, t_new, pages_per_step)
        br_lru, h_last, conv_new = _lru(z_rest, state_conv[l], h_s0[l], *lru_w_args, l, bs, t_new, t_new, F32)
        br_mem = _mem_attention(z_rest, mem_k_s[l], mem_v_s[l], bs, t_new, t_new, F32)
        merged = _merge([br_att, br_lru, br_mem], z_rest, w_branch, l, ms, 512)
        xs = _matmul(merged, w_out, l, 0, d, ms, 1024, False, residual=xs)
        st_s.append((k.reshape(bs, t_new, KV_HEADS, HEAD_DIM), v.reshape(bs, t_new, KV_HEADS, HEAD_DIM),
                     ki.reshape(bs, t_new, IDX_DIM), conv_new, h_last.reshape(bs, lru_w)))

    g_f = final_norm_g.reshape(1, d)
    y_prompt = _rmsnorm(xp, g_f, _pick_tile(mp, 512)).reshape(bp, seq, d)
    y_sample = _rmsnorm(xs, g_f, ms).reshape(bs, t_new, d)
    k_p, v_p, ik_p, conv_p, h_p = [jnp.stack(t) for t in zip(*st_p)]
    k_s, v_s, ik_s, conv_s, h_s = [jnp.stack(t) for t in zip(*st_s)]
    return (y_prompt, y_sample, k_p, v_p, ik_p, conv_p, h_p, jnp.stack(mk_list), jnp.stack(mv_list),
            k_s, v_s, ik_s, conv_s, h_s)
```

```python
import functools

import jax
import jax.numpy as jnp
from jax import lax
from jax.experimental import pallas as pl
from jax.experimental.pallas import tpu as pltpu

F32 = jnp.float32
BF16 = jnp.bfloat16
I32 = jnp.int32

N_HEADS = 8
HEAD_DIM = 128
KV_HEADS = 2
HEADS_PER_KV = N_HEADS // KV_HEADS
IDX_HEADS = 8
IDX_DIM = 64
INDEX_TOPK = 256
LRU_BLOCKS = 8
CONV_W = 4
LRU_C = 8.0
MEM_HEADS = 4
N_BRANCH = 3
ROPE_THETA = 10000.0
NORM_EPS = 1e-6

ATTN_W = N_HEADS * HEAD_DIM
KV_W = KV_HEADS * HEAD_DIM
QI_W = IDX_HEADS * IDX_DIM

LANES = 128
SUBLANES = 8
VMEM_LIMIT = 52 * 1024 * 1024
CAST_ROWS = 256
COUNT_ROWS = 8 * SUBLANES
LOG2E = 1.4426950408889634
QK_SCALE = HEAD_DIM ** -0.5 * LOG2E

NEG_BIG = -1e30
INT_MIN = -(2 ** 31)


def _cparams(n_axes):
    return pltpu.CompilerParams(dimension_semantics=("arbitrary",) * n_axes,
                                vmem_limit_bytes=VMEM_LIMIT)


def _sigmoid(x):
    return 1.0 / (1.0 + jnp.exp(-x))


def _silu(x):
    return x * _sigmoid(x)


def _dot(a, b):
    return jnp.dot(a, b, preferred_element_type=F32)


def _dot_nt(a, b):
    return lax.dot_general(a, b, (((1,), (1,)), ((), ())), preferred_element_type=F32)


def _rms(x, g):
    ms = jnp.mean(x * x, axis=-1, keepdims=True)
    return (x * lax.rsqrt(ms + NORM_EPS)) * g


def _cast_weight(dst_ref, src_ref):
    k = dst_ref.shape[0]
    for r in range(0, k, CAST_ROWS):
        rows = slice(r, min(r + CAST_ROWS, k))
        dst_ref[rows, :] = src_ref[rows, :].astype(BF16)


def _proj_attn_kernel(x_ref, g_ref, w_ref, cq_ref, sq_ref, ci_ref, si_ref,
                      xn_ref, k_ref, v_ref, ki_ref, kb_ref, kib_ref, q_ref, qi_ref, wi_ref, vb_ref,
                      wbf_ref, *, layer, transposed):
    @pl.when(pl.program_id(0) == 0)
    def _():
        _cast_weight(wbf_ref, w_ref)

    xnb = _rms(x_ref[...], g_ref[layer:layer + 1, :]).astype(BF16)
    xn_ref[...] = xnb
    z = _dot_nt(xnb, wbf_ref[...])

    cq, sq, ci, si = cq_ref[...], sq_ref[...], ci_ref[...], si_ref[...]
    lane = lax.broadcasted_iota(I32, cq.shape, 1)
    first_half = (lane & (IDX_DIM // 2)) == 0

    def rope128(t):
        return t * cq + pltpu.roll(t, HEAD_DIM // 2, 1) * sq

    def rope64(t):
        rot = jnp.where(first_half, pltpu.roll(t, LANES - IDX_DIM // 2, 1),
                        pltpu.roll(t, IDX_DIM // 2, 1))
        return t * ci + rot * si

    def tile(c):
        return z[:, c * LANES:(c + 1) * LANES]

    for h in range(N_HEADS):
        qh = rope128(tile(h)) * QK_SCALE
        if transposed:
            q_ref[h * HEAD_DIM:(h + 1) * HEAD_DIM, :] = qh.T.astype(BF16)
        else:
            q_ref[:, h * HEAD_DIM:(h + 1) * HEAD_DIM] = qh
    c0 = ATTN_W // LANES
    for h in range(KV_HEADS):
        sl = slice(h * HEAD_DIM, (h + 1) * HEAD_DIM)
        kr = rope128(tile(c0 + h))
        k_ref[:, sl] = kr
        kb_ref[:, sl] = kr.astype(BF16)
    c0 += KV_W // LANES
    for h in range(KV_HEADS):
        sl = slice(h * HEAD_DIM, (h + 1) * HEAD_DIM)
        vh = tile(c0 + h)
        v_ref[:, sl] = vh
        if transposed:
            vb_ref[sl, :] = vh.T.astype(BF16)
        else:
            vb_ref[:, sl] = vh.astype(BF16)
    c0 += KV_W // LANES
    for c in range(QI_W // LANES):
        qc = rope64(tile(c0 + c))
        if transposed:
            qi_ref[c * LANES:(c + 1) * LANES, :] = qc.T.astype(BF16)
        else:
            qi_ref[:, c * LANES:(c + 1) * LANES] = qc.astype(BF16)
    c0 += QI_W // LANES
    last = tile(c0)
    kir = rope64(last)[:, :IDX_DIM]
    ki_ref[...] = kir
    kib_ref[...] = kir.astype(BF16)
    if transposed:
        wi_ref[...] = last.T[IDX_DIM:IDX_DIM + IDX_HEADS, :] * (IDX_HEADS ** -0.5)
    else:
        wi_ref[...] = last[:, IDX_DIM:IDX_DIM + IDX_HEADS] * (IDX_HEADS ** -0.5)


def _proj_attn(x, norm_g, w_in_t, tabs, layer, tm, transposed):
    m, d = x.shape
    na = ATTN_W + 2 * KV_W + QI_W + LANES
    row = lambda i: (i, 0)
    col = lambda i: (0, i)
    const = lambda i: (0, 0)

    def tok(width, dtype):
        return ((m, width), dtype, pl.BlockSpec((tm, width), row))

    def feat(width, dtype):
        return ((width, m), dtype, pl.BlockSpec((width, tm), col))

    outs = [tok(d, BF16), tok(KV_W, F32), tok(KV_W, F32), tok(IDX_DIM, F32), tok(KV_W, BF16), tok(IDX_DIM, BF16)]
    if transposed:
        outs += [feat(ATTN_W, BF16), feat(QI_W, BF16), feat(IDX_HEADS, F32), feat(KV_W, BF16)]
    else:
        outs += [tok(ATTN_W, F32), tok(QI_W, BF16), tok(IDX_HEADS, F32), tok(KV_W, BF16)]
    return pl.pallas_call(
        functools.partial(_proj_attn_kernel, layer=layer, transposed=transposed),
        grid=(m // tm,),
        in_specs=[pl.BlockSpec((tm, d), row), pl.BlockSpec(norm_g.shape, const),
                  pl.BlockSpec((None, na, d), lambda i: (layer, 0, 0), pipeline_mode=pl.Buffered(1))]
                 + [pl.BlockSpec((tm, LANES), row)] * 4,
        out_specs=[o[2] for o in outs],
        out_shape=[jax.ShapeDtypeStruct(o[0], o[1]) for o in outs],
        scratch_shapes=[pltpu.VMEM((na, d), BF16)],
        compiler_params=_cparams(1),
        name="proj_attn",
    )(x, norm_g, w_in_t, *tabs)


def _mm_kernel(*refs, w_transposed, residual):
    x_ref, w_ref = refs[0], refs[1]
    r_ref = refs[2] if residual else None
    o_ref, wbf_ref = refs[-2], refs[-1]

    @pl.when(pl.program_id(1) == 0)
    def _():
        _cast_weight(wbf_ref, w_ref.at[0] if w_transposed else w_ref)

    out = _dot_nt(x_ref[...], wbf_ref[...]) if w_transposed else _dot(x_ref[...], wbf_ref[...])
    if residual:
        out = r_ref[...] + out
    o_ref[...] = out


def _matmul(x, w, layer, col0, n, tm, tn, w_transposed, residual=None):
    m, k = x.shape
    assert n % tn == 0 and m % tm == 0
    if w_transposed:
        assert col0 % SUBLANES == 0
        w_spec = pl.BlockSpec((pl.Element(1), pl.Element(tn), pl.Element(k)),
                              lambda j, i: (layer, (col0 // SUBLANES + j * (tn // SUBLANES)) * SUBLANES, 0))
        w_scratch = pltpu.VMEM((tn, k), BF16)
    else:
        assert col0 % tn == 0
        w_spec = pl.BlockSpec((None, k, tn), lambda j, i: (layer, 0, col0 // tn + j))
        w_scratch = pltpu.VMEM((k, tn), BF16)
    in_specs = [pl.BlockSpec((tm, k), lambda j, i: (i, 0)), w_spec]
    args = [x, w]
    if residual is not None:
        in_specs.append(pl.BlockSpec((tm, tn), lambda j, i: (i, j)))
        args.append(residual)
    return pl.pallas_call(
        functools.partial(_mm_kernel, w_transposed=w_transposed, residual=residual is not None),
        grid=(n // tn, m // tm),
        in_specs=in_specs,
        out_specs=pl.BlockSpec((tm, tn), lambda j, i: (i, j)),
        out_shape=jax.ShapeDtypeStruct((m, n), F32),
        scratch_shapes=[w_scratch],
        compiler_params=_cparams(2),
        name="matmul_residual" if residual is not None else "matmul",
    )(*args)


def _norm_mm_kernel(x_ref, g_ref, w_ref, o_ref, *, layer):
    xn = _rms(x_ref[...], g_ref[layer:layer + 1, :]).astype(BF16)
    o_ref[...] = _dot(xn, w_ref[...].astype(BF16))


def _norm_matmul(x, g, w, layer, tm, tn):
    m, k = x.shape
    n = w.shape[2]
    return pl.pallas_call(
        functools.partial(_norm_mm_kernel, layer=layer),
        grid=(n // tn, m // tm),
        in_specs=[pl.BlockSpec((tm, k), lambda j, i: (i, 0)), pl.BlockSpec(g.shape, lambda j, i: (0, 0)),
                  pl.BlockSpec((None, k, tn), lambda j, i: (layer, 0, j))],
        out_specs=pl.BlockSpec((tm, tn), lambda j, i: (i, j)),
        out_shape=jax.ShapeDtypeStruct((m, n), F32),
        compiler_params=_cparams(2),
        name="norm_matmul",
    )(x, g, w)


def _rmsnorm_kernel(x_ref, g_ref, o_ref):
    o_ref[...] = _rms(x_ref[...], g_ref[...])


def _rmsnorm(x, g, tm):
    m, d = x.shape
    return pl.pallas_call(
        _rmsnorm_kernel,
        grid=(m // tm,),
        in_specs=[pl.BlockSpec((tm, d), lambda i: (i, 0)), pl.BlockSpec((1, d), lambda i: (0, 0))],
        out_specs=pl.BlockSpec((tm, d), lambda i: (i, 0)),
        out_shape=jax.ShapeDtypeStruct((m, d), F32),
        compiler_params=_cparams(1),
        name="final_rmsnorm",
    )(x, g)


def _sortable_key(score):
    bits = pltpu.bitcast(score, I32)
    return bits ^ ((bits >> 31) & jnp.int32(0x7FFFFFFF))


def _kth_largest_key(count_ge, shape, topk):
    zero = jnp.zeros(shape, I32)
    ans = jnp.where(count_ge(zero) >= topk, zero, jnp.full(shape, INT_MIN, I32))

    def bit_body(j, ans):
        cand = ans | lax.shift_left(jnp.int32(1), 30 - j)
        return jnp.where(count_ge(cand) >= topk, cand, ans)

    return lax.fori_loop(0, 31, bit_body, ans)


def _tie_cutoff(count_tie_lt, need, shape, idx_bits):
    def bit_body(j, cut):
        cand = cut | lax.shift_left(jnp.int32(1), idx_bits - 1 - j)
        return jnp.where(count_tie_lt(cand) < need, cand, cut)

    return lax.fori_loop(0, idx_bits, bit_body, jnp.zeros(shape, I32))


def _prompt_attn_kernel(qi_ref, wi_ref, kib_ref, q_ref, kb_ref, vb_ref, ga_ref, o_ref,
                        key_ref, bias_ref, acc_ref, *, tq, kc, topk, seq):
    i = pl.program_id(1)
    nch = ((i + 1) * tq + kc - 1) // kc
    key_pos = lax.broadcasted_iota(I32, (kc, tq), 0)
    q_pos = i * tq + lax.broadcasted_iota(I32, (kc, tq), 1)
    idx_bits = max(1, (seq - 1).bit_length())
    row1 = (1, tq)

    def chunk(ref, c):
        return ref[pl.ds(pl.multiple_of(c * kc, kc), kc), :]

    wi = wi_ref[...] * (IDX_DIM ** -0.5)

    def score_body(c, carry):
        kic = chunk(kib_ref, c)
        acc = jnp.zeros((kc, tq), F32)
        for h in range(IDX_HEADS):
            s = _dot(kic, qi_ref[h * IDX_DIM:(h + 1) * IDX_DIM, :])
            acc = acc + jnp.maximum(s, 0.0) * wi[h:h + 1, :]
        key = jnp.where(c * kc + key_pos <= q_pos, _sortable_key(acc), INT_MIN)
        key_ref[pl.ds(pl.multiple_of(c * kc, kc), kc), :] = key
        return carry

    lax.fori_loop(0, nch, score_body, 0)

    def chunk_count(pred):
        def body(c, acc):
            ind = jnp.where(pred(chunk(key_ref, c), c * kc + key_pos), 1.0, 0.0)
            return acc + jnp.sum(ind.reshape(kc // COUNT_ROWS, COUNT_ROWS, tq), axis=0)
        acc = lax.fori_loop(0, nch, body, jnp.zeros((COUNT_ROWS, tq), F32))
        return jnp.sum(acc, axis=0, keepdims=True)

    def count_ge(cand):
        return chunk_count(lambda k, pos: k >= cand)

    def write_bias(keep):
        def body(c, carry):
            k = chunk(key_ref, c)
            bias_ref[pl.ds(pl.multiple_of(c * kc, kc), kc), :] = jnp.where(keep(k, c * kc + key_pos), 0.0, NEG_BIG)
            return carry
        lax.fori_loop(0, nch, body, 0)

    need_search = (i + 1) * tq > topk
    thr = lax.cond(need_search,
                   lambda: jnp.maximum(_kth_largest_key(count_ge, row1, topk), INT_MIN + 1),
                   lambda: jnp.full(row1, INT_MIN + 1, I32))
    write_bias(lambda k, pos: k >= thr)

    @pl.when(jnp.max(count_ge(thr)) > topk)
    def _():
        need = topk - count_ge(thr + 1)
        count_tie_lt = lambda cut: chunk_count(lambda k, pos: jnp.where(k == thr, pos, seq) < cut)
        cut = _tie_cutoff(count_tie_lt, need, row1, idx_bits)
        write_bias(lambda k, pos: (k > thr) | ((k == thr) & (pos <= cut)))

    qw = HEADS_PER_KV * tq
    q4 = [jnp.concatenate([q_ref[(g * HEADS_PER_KV + hh) * HEAD_DIM:(g * HEADS_PER_KV + hh + 1) * HEAD_DIM, :]
                           for hh in range(HEADS_PER_KV)], axis=1) for g in range(KV_HEADS)]
    acc_ref[...] = jnp.zeros(acc_ref.shape, F32)

    def attn_body(c, carry):
        off = pl.multiple_of(c * kc, kc)
        bias = chunk(bias_ref, c)
        bias4 = jnp.concatenate([bias] * HEADS_PER_KV, axis=1)
        out = []
        for g in range(KV_HEADS):
            m_prev, l_prev = carry[g]
            hs = slice(g * HEAD_DIM, (g + 1) * HEAD_DIM)
            s = _dot(kb_ref[pl.ds(off, kc), hs], q4[g]) + bias4
            m_next = jnp.maximum(m_prev, jnp.max(s, axis=0, keepdims=True))
            p = jnp.exp2(s - m_next)
            alpha = jnp.exp2(m_prev - m_next)
            l_next = alpha * l_prev + jnp.sum(p, axis=0, keepdims=True)
            acc_ref[g] = acc_ref[g] * alpha + _dot(vb_ref[hs, pl.ds(off, kc)], p.astype(BF16))
            out.append((m_next, l_next))
        return tuple(out)

    init = tuple((jnp.full((1, qw), NEG_BIG, F32), jnp.zeros((1, qw), F32)) for _ in range(KV_HEADS))
    stats = lax.fori_loop(0, nch, attn_body, init)
    for g in range(KV_HEADS):
        o_t = acc_ref[g] / stats[g][1]
        for hh in range(HEADS_PER_KV):
            sl = slice((g * HEADS_PER_KV + hh) * HEAD_DIM, (g * HEADS_PER_KV + hh + 1) * HEAD_DIM)
            o_ref[:, sl] = (o_t[:, hh * tq:(hh + 1) * tq].T * _silu(ga_ref[:, sl])).astype(BF16)


def _prompt_attention(qi_t, wi_t, kib, q_t, kb, vb_t, z_rest, batch, seq, tq=128):
    nq = seq // tq
    kc = _pick_tile(seq, 512)
    topk = min(INDEX_TOPK, seq // 4)
    blk_t = lambda b, i: (0, b * nq + i)
    blk = lambda b, i: (b * nq + i, 0)
    return pl.pallas_call(
        functools.partial(_prompt_attn_kernel, tq=tq, kc=kc, topk=topk, seq=seq),
        grid=(batch, nq),
        in_specs=[pl.BlockSpec((QI_W, tq), blk_t), pl.BlockSpec((IDX_HEADS, tq), blk_t),
                  pl.BlockSpec((seq, IDX_DIM), lambda b, i: (b, 0)), pl.BlockSpec((ATTN_W, tq), blk_t),
                  pl.BlockSpec((seq, KV_W), lambda b, i: (b, 0)), pl.BlockSpec((KV_W, seq), lambda b, i: (0, b)),
                  pl.BlockSpec((tq, ATTN_W), blk)],
        out_specs=pl.BlockSpec((tq, ATTN_W), blk),
        out_shape=jax.ShapeDtypeStruct((batch * seq, ATTN_W), BF16),
        scratch_shapes=[pltpu.VMEM((seq, tq), I32), pltpu.VMEM((seq, tq), F32),
                        pltpu.VMEM((KV_HEADS, HEAD_DIM, HEADS_PER_KV * tq), F32)],
        compiler_params=_cparams(2),
        name="prompt_attention",
    )(qi_t, wi_t, kib, q_t, kb, vb_t, z_rest)


def _sample_index_kernel(pt_ref, qs_ref, ws_ref, *rest, pages_per_step, t_new):
    page_refs, o_ref = rest[:pages_per_step], rest[pages_per_step]
    qs = qs_ref[...]
    ws = ws_ref[...] * (IDX_DIM ** -0.5)
    page = page_refs[0].shape[1]
    wsb = jnp.broadcast_to(ws, (t_new * IDX_HEADS, page))
    for j in range(pages_per_step):
        kp = page_refs[j][...].astype(BF16)
        s = jnp.maximum(_dot(qs, kp), 0.0) * wsb
        o_ref[:, j * page:(j + 1) * page] = jnp.sum(s.reshape(t_new, IDX_HEADS, page), axis=1)


def _sample_index(page_table, qs, ws, pool_ik_t, layer, dec_batch, t_new, pages_per_step):
    n_pages = page_table.shape[1]
    page = pool_ik_t.shape[3]
    steps = n_pages // pages_per_step
    rows = t_new * IDX_HEADS

    def page_map(j):
        return lambda b, s, pt: (layer, pt[b, s * pages_per_step + j], 0, 0)

    grid_spec = pltpu.PrefetchScalarGridSpec(
        num_scalar_prefetch=1,
        grid=(dec_batch, steps),
        in_specs=[pl.BlockSpec((rows, IDX_DIM), lambda b, s, pt: (b, 0)),
                  pl.BlockSpec((rows, 1), lambda b, s, pt: (b, 0))]
                 + [pl.BlockSpec((None, None, IDX_DIM, page), page_map(j)) for j in range(pages_per_step)],
        out_specs=pl.BlockSpec((t_new, pages_per_step * page), lambda b, s, pt: (b, s)),
    )
    return pl.pallas_call(
        functools.partial(_sample_index_kernel, pages_per_step=pages_per_step, t_new=t_new),
        grid_spec=grid_spec,
        out_shape=jax.ShapeDtypeStruct((dec_batch * t_new, n_pages * page), F32),
        compiler_params=_cparams(2),
        name="sample_index",
    )(page_table, qs, ws, *([pool_ik_t] * pages_per_step))


def _sample_select_kernel(sc_ref, qs_ref, ws_ref, kn_ref, bias_ref, bias_new_ref, key_ref, keyn_ref,
                          *, dec_batch, t_new, past, topk):
    rows = dec_batch * t_new
    hrows = t_new * IDX_HEADS
    ws = ws_ref[...] * (IDX_DIM ** -0.5)
    lane_n = lax.broadcasted_iota(I32, (rows, LANES), 1)
    row_n = lax.broadcasted_iota(I32, (rows, LANES), 0) % t_new
    s_new = []
    for b in range(dec_batch):
        hs = slice(b * hrows, (b + 1) * hrows)
        s = jnp.maximum(_dot_nt(qs_ref[hs, :], kn_ref[b]), 0.0) * jnp.broadcast_to(ws[hs, :], (hrows, LANES))
        s_new.append(jnp.sum(s.reshape(t_new, IDX_HEADS, LANES), axis=1))
    s_new = jnp.concatenate(s_new, axis=0) + 0.0
    keyn_ref[...] = jnp.where(lane_n <= row_n, _sortable_key(s_new), INT_MIN)
    key_ref[...] = _sortable_key(sc_ref[...] + 0.0)
    pos = lax.broadcasted_iota(I32, (rows, past), 1)
    idx_bits = (past + LANES - 1).bit_length()
    col1 = (rows, 1)

    def count(pred):
        a = jnp.sum(jnp.where(pred(key_ref[...], pos), 1.0, 0.0), axis=1, keepdims=True)
        b = jnp.sum(jnp.where(pred(keyn_ref[...], past + lane_n), 1.0, 0.0), axis=1, keepdims=True)
        return a + b

    count_ge = lambda cand: count(lambda k, p: k >= cand)
    thr = jnp.maximum(_kth_largest_key(count_ge, col1, topk), INT_MIN + 1)
    bias_ref[...] = jnp.where(key_ref[...] >= thr, 0.0, NEG_BIG)
    bias_new_ref[...] = jnp.where(keyn_ref[...] >= thr, 0.0, NEG_BIG)

    @pl.when(jnp.max(count_ge(thr)) > topk)
    def _():
        need = topk - count_ge(thr + 1)
        big = jnp.int32(2 ** idx_bits)
        count_tie_lt = lambda cut: count(lambda k, p: jnp.where(k == thr, p, big) < cut)
        cut = _tie_cutoff(count_tie_lt, need, col1, idx_bits)
        keep = lambda k, p: (k > thr) | ((k == thr) & (p <= cut))
        bias_ref[...] = jnp.where(keep(key_ref[...], pos), 0.0, NEG_BIG)
        bias_new_ref[...] = jnp.where(keep(keyn_ref[...], past + lane_n), 0.0, NEG_BIG)


def _sample_select(sc, qs, ws, knew_pad, dec_batch, t_new, topk):
    rows, past = sc.shape
    whole = lambda a: pl.BlockSpec(a.shape, lambda i: (0,) * a.ndim)
    out_shape = [jax.ShapeDtypeStruct((rows, past), F32), jax.ShapeDtypeStruct((rows, LANES), F32)]
    return pl.pallas_call(
        functools.partial(_sample_select_kernel, dec_batch=dec_batch, t_new=t_new, past=past, topk=topk),
        grid=(1,),
        in_specs=[whole(sc), whole(qs), whole(ws), whole(knew_pad)],
        out_specs=[whole(o) for o in out_shape],
        out_shape=out_shape,
        scratch_shapes=[pltpu.VMEM((rows, past), I32), pltpu.VMEM((rows, LANES), I32)],
        compiler_params=_cparams(1),
        name="sample_select",
    )(sc, qs, ws, knew_pad)


def _sample_attn_kernel(pt_ref, q_ref, bias_ref, biasn_ref, kn_ref, vn_ref, ga_ref, *rest,
                        pages_per_step, t_new):
    k_refs = rest[:pages_per_step]
    v_refs = rest[pages_per_step:2 * pages_per_step]
    o_ref, m_ref, l_ref, acc_ref = rest[2 * pages_per_step:]
    step = pl.program_id(1)
    page = k_refs[0].shape[0] // KV_HEADS

    @pl.when(step == 0)
    def _():
        m_ref[...] = jnp.full(m_ref.shape, NEG_BIG, F32)
        l_ref[...] = jnp.zeros(l_ref.shape, F32)
        acc_ref[...] = jnp.zeros(acc_ref.shape, F32)

    def q_group(g):
        return jnp.concatenate(
            [q_ref[:, (g * HEADS_PER_KV + hh) * HEAD_DIM:(g * HEADS_PER_KV + hh + 1) * HEAD_DIM]
             for hh in range(HEADS_PER_KV)], axis=0).astype(BF16)

    def head_rows(ref, g):
        return ref[pl.ds(g, page, stride=KV_HEADS), :].astype(BF16)

    def update(g, s, v_of):
        m_prev = m_ref[g]
        m_next = jnp.maximum(m_prev, jnp.max(s, axis=1, keepdims=True))
        n = s.shape[1] // LANES
        p = jnp.exp2(s - jnp.concatenate([m_next] * n, axis=1))
        alpha = jnp.exp2(m_prev - m_next)
        l_ref[g] = alpha * l_ref[g] + jnp.sum(p, axis=1, keepdims=True)
        pv = _dot(p[:, :page].astype(BF16), v_of(0))
        for j in range(1, s.shape[1] // page):
            pv = pv + _dot(p[:, j * page:(j + 1) * page].astype(BF16), v_of(j))
        acc_ref[g] = acc_ref[g] * alpha + pv
        m_ref[g] = m_next

    bias4 = jnp.concatenate([bias_ref[...]] * HEADS_PER_KV, axis=0)
    for g in range(KV_HEADS):
        qg = q_group(g)
        s = jnp.concatenate([_dot_nt(qg, head_rows(k_refs[j], g)) for j in range(pages_per_step)], axis=1)
        update(g, s + bias4, lambda j: head_rows(v_refs[j], g))

    @pl.when(step == pl.num_programs(1) - 1)
    def _():
        biasn4 = jnp.concatenate([biasn_ref[...]] * HEADS_PER_KV, axis=0)
        for g in range(KV_HEADS):
            hs = slice(g * HEAD_DIM, (g + 1) * HEAD_DIM)
            s = _dot_nt(q_group(g), kn_ref[:, hs]) + biasn4
            update(g, s, lambda j: vn_ref[:, hs])
            o = acc_ref[g] / l_ref[g]
            for hh in range(HEADS_PER_KV):
                sl = slice((g * HEADS_PER_KV + hh) * HEAD_DIM, (g * HEADS_PER_KV + hh + 1) * HEAD_DIM)
                o_ref[:, sl] = o[hh * t_new:(hh + 1) * t_new] * _silu(ga_ref[:, sl])


def _sample_attention(page_table, q, bias, bias_new, knew_pad, vnew_pad, z_rest, pool_k, pool_v, layer,
                      dec_batch, t_new, pages_per_step):
    n_pages = page_table.shape[1]
    page_rows = pool_k.shape[2]
    page = page_rows // KV_HEADS
    steps = n_pages // pages_per_step
    rows = HEADS_PER_KV * t_new
    per_b = lambda b, s, pt: (b, 0)

    def page_map(j):
        return lambda b, s, pt: (layer, pt[b, s * pages_per_step + j], 0, 0)

    page_specs = [pl.BlockSpec((None, None, page_rows, HEAD_DIM), page_map(j)) for j in range(pages_per_step)]
    grid_spec = pltpu.PrefetchScalarGridSpec(
        num_scalar_prefetch=1,
        grid=(dec_batch, steps),
        in_specs=[pl.BlockSpec((t_new, ATTN_W), per_b),
                  pl.BlockSpec((t_new, pages_per_step * page), lambda b, s, pt: (b, s)),
                  pl.BlockSpec((t_new, LANES), per_b),
                  pl.BlockSpec((None, LANES, KV_W), lambda b, s, pt: (b, 0, 0)),
                  pl.BlockSpec((None, LANES, KV_W), lambda b, s, pt: (b, 0, 0)),
                  pl.BlockSpec((t_new, ATTN_W), per_b)] + page_specs + page_specs,
        out_specs=pl.BlockSpec((t_new, ATTN_W), per_b),
        scratch_shapes=[pltpu.VMEM((KV_HEADS, rows, LANES), F32), pltpu.VMEM((KV_HEADS, rows, LANES), F32),
                        pltpu.VMEM((KV_HEADS, rows, HEAD_DIM), F32)],
    )
    return pl.pallas_call(
        functools.partial(_sample_attn_kernel, pages_per_step=pages_per_step, t_new=t_new),
        grid_spec=grid_spec,
        out_shape=jax.ShapeDtypeStruct((dec_batch * t_new, ATTN_W), F32),
        compiler_params=_cparams(2),
        name="sample_attention",
    )(page_table, q, bias, bias_new, knew_pad, vnew_pad, z_rest,
      *([pool_k] * pages_per_step), *([pool_v] * pages_per_step))


def _lru_kernel(xr_ref, gr_ref, cp_ref, hp_ref, cw_ref, cb_ref, wr_ref, br_ref, wg_ref, bg_ref, lam_ref,
                o_ref, hl_ref, cn_ref, ext_ref, hc_ref, a_ref, b_ref, *, tt, layer):
    t = pl.program_id(1)
    tail = CONV_W - 1
    base = SUBLANES
    width = xr_ref.shape[1]
    bw = width // LRU_BLOCKS
    vec = lambda ref: ref[layer:layer + 1, :]

    @pl.when(t == 0)
    def _():
        ext_ref[base - tail:base, :] = cp_ref[...]
        hc_ref[...] = hp_ref[...]

    x = xr_ref[...]
    ext_ref[base:base + tt, :] = x
    cw = cw_ref[...]
    xc = vec(cb_ref) + x * cw[tail:tail + 1, :]
    for j in range(tail):
        xc = xc + ext_ref[base - tail + j: base - tail + j + tt, :] * cw[j:j + 1, :]
    new_tail = x[tt - tail:, :]
    ext_ref[base - tail:base, :] = new_tail

    xcb = xc.astype(BF16)
    r_lin = jnp.concatenate([_dot(xcb[:, n * bw:(n + 1) * bw], wr_ref[n].astype(BF16))
                             for n in range(LRU_BLOCKS)], axis=1)
    g_lin = jnp.concatenate([_dot(xcb[:, n * bw:(n + 1) * bw], wg_ref[n].astype(BF16))
                             for n in range(LRU_BLOCKS)], axis=1)
    r = _sigmoid(r_lin + vec(br_ref))
    gi = _sigmoid(g_lin + vec(bg_ref))
    nl = -vec(lam_ref)
    softplus = jnp.maximum(nl, 0.0) + jnp.log1p(jnp.exp(-jnp.abs(nl)))
    log_a = (-LRU_C) * r * softplus
    a = jnp.exp(log_a)
    b = jnp.sqrt(jnp.tanh(-log_a) * (a * a + 1.0)) * (gi * xc)

    row = lax.broadcasted_iota(I32, (tt, width), 0) & (SUBLANES - 1)
    d = 1
    while d < SUBLANES:
        keep = row >= d
        a_s = jnp.where(keep, pltpu.roll(a, d, 0), 1.0)
        b_s = jnp.where(keep, pltpu.roll(b, d, 0), 0.0)
        b = a * b_s + b
        a = a * a_s
        d *= 2
    a_ref[...] = a
    b_ref[...] = b

    def group_body(gidx, h):
        off = pl.multiple_of(gidx * SUBLANES, SUBLANES)
        h8 = a_ref[pl.ds(off, SUBLANES), :] * h + b_ref[pl.ds(off, SUBLANES), :]
        b_ref[pl.ds(off, SUBLANES), :] = h8
        return h8[SUBLANES - 1:SUBLANES, :]

    h_last = lax.fori_loop(0, tt // SUBLANES, group_body, hc_ref[...])
    hc_ref[...] = h_last
    o_ref[...] = (b_ref[...] * _silu(gr_ref[...])).astype(o_ref.dtype)

    @pl.when(t == pl.num_programs(1) - 1)
    def _():
        hl_ref[...] = h_last
        cn_ref[...] = new_tail


def _lru(z_rest, conv_prev, h_prev, conv_w, conv_b, w_r, b_r, w_g, b_g, lam, layer, batch, t_len, tt, out_dtype):
    width = conv_w.shape[2]
    nt = t_len // tt
    tail = CONV_W - 1
    bw = width // LRU_BLOCKS
    whole = lambda a: pl.BlockSpec(a.shape, lambda b, t: (0,) * a.ndim)
    per_b = lambda b, t: (b, 0, 0)
    return pl.pallas_call(
        functools.partial(_lru_kernel, tt=tt, layer=layer),
        grid=(batch, nt),
        in_specs=[pl.BlockSpec((tt, width), lambda b, t: (b * nt + t, 1)),
                  pl.BlockSpec((tt, width), lambda b, t: (b * nt + t, 2)),
                  pl.BlockSpec((None, tail, width), per_b), pl.BlockSpec((None, 1, width), per_b),
                  pl.BlockSpec((None, CONV_W, width), lambda b, t: (layer, 0, 0)), whole(conv_b),
                  pl.BlockSpec((None, LRU_BLOCKS, bw, bw), lambda b, t: (layer, 0, 0, 0)), whole(b_r),
                  pl.BlockSpec((None, LRU_BLOCKS, bw, bw), lambda b, t: (layer, 0, 0, 0)), whole(b_g),
                  whole(lam)],
        out_specs=[pl.BlockSpec((tt, width), lambda b, t: (b * nt + t, 0)),
                   pl.BlockSpec((None, 1, width), per_b), pl.BlockSpec((None, tail, width), per_b)],
        out_shape=[jax.ShapeDtypeStruct((batch * t_len, width), out_dtype),
                   jax.ShapeDtypeStruct((batch, 1, width), F32),
                   jax.ShapeDtypeStruct((batch, tail, width), F32)],
        scratch_shapes=[pltpu.VMEM((SUBLANES + tt, width), F32), pltpu.VMEM((1, width), F32),
                        pltpu.VMEM((tt, width), F32), pltpu.VMEM((tt, width), F32)],
        compiler_params=_cparams(2),
        name="rglru",
    )(z_rest, z_rest, conv_prev, h_prev, conv_w, conv_b, w_r, b_r, w_g, b_g, lam)


def _mem_attn_kernel(qm_ref, gm_ref, mk_ref, mv_ref, o_ref):
    width = qm_ref.shape[1]
    hd = width // MEM_HEADS
    scale = hd ** -0.5
    for h in range(MEM_HEADS):
        sl = slice(h * hd, (h + 1) * hd)
        s = _dot_nt(qm_ref[:, sl].astype(BF16), mk_ref[:, sl].astype(BF16)) * scale
        p = jnp.exp(s - jnp.max(s, axis=1, keepdims=True))
        l = jnp.sum(p, axis=1, keepdims=True)
        o = _dot(p.astype(BF16), mv_ref[:, sl].astype(BF16)) / l
        o_ref[:, sl] = (o * _silu(gm_ref[:, sl])).astype(o_ref.dtype)


def _mem_attention(z_rest, mk, mv, batch, t_len, tm, out_dtype):
    width = mk.shape[2]
    mem = mk.shape[1]
    nt = t_len // tm
    return pl.pallas_call(
        _mem_attn_kernel,
        grid=(batch, nt),
        in_specs=[pl.BlockSpec((tm, width), lambda b, t: (b * nt + t, 3)),
                  pl.BlockSpec((tm, width), lambda b, t: (b * nt + t, 4)),
                  pl.BlockSpec((None, mem, width), lambda b, t: (b, 0, 0)),
                  pl.BlockSpec((None, mem, width), lambda b, t: (b, 0, 0))],
        out_specs=pl.BlockSpec((tm, width), lambda b, t: (b * nt + t, 0)),
        out_shape=jax.ShapeDtypeStruct((batch * t_len, width), out_dtype),
        compiler_params=_cparams(2),
        name="mem_attention",
    )(z_rest, z_rest, mk, mv)


def _merge_kernel(ba_ref, bl_ref, bm_ref, ga_ref, gl_ref, gm_ref, wa_ref, wl_ref, wm_ref, o_ref, wbf_ref):
    @pl.when(pl.program_id(1) == 0)
    def _():
        for b, w_ref in enumerate((wa_ref, wl_ref, wm_ref)):
            _cast_weight(wbf_ref.at[b], w_ref)

    out = _sigmoid(ga_ref[...]) * _dot(ba_ref[...].astype(BF16), wbf_ref[0])
    out = out + _sigmoid(gl_ref[...]) * _dot(bl_ref[...].astype(BF16), wbf_ref[1])
    out = out + _sigmoid(gm_ref[...]) * _dot(bm_ref[...].astype(BF16), wbf_ref[2])
    o_ref[...] = out.astype(BF16)


def _merge(branches, z_rest, w_branch, layer, tm, tn):
    m, bw = branches[0].shape
    d = w_branch.shape[3]
    gate_col0 = 5 * bw // tn
    br_spec = pl.BlockSpec((tm, bw), lambda j, i: (i, 0))

    def gate_spec(b):
        return pl.BlockSpec((tm, tn), lambda j, i: (i, gate_col0 + b * (d // tn) + j))

    def w_spec(b):
        return pl.BlockSpec((None, None, bw, tn), lambda j, i: (layer, b, 0, j))

    return pl.pallas_call(
        _merge_kernel,
        grid=(d // tn, m // tm),
        in_specs=[br_spec] * N_BRANCH + [gate_spec(b) for b in range(N_BRANCH)]
                 + [w_spec(b) for b in range(N_BRANCH)],
        out_specs=pl.BlockSpec((tm, tn), lambda j, i: (i, j)),
        out_shape=jax.ShapeDtypeStruct((m, d), BF16),
        scratch_shapes=[pltpu.VMEM((N_BRANCH, bw, tn), BF16)],
        compiler_params=_cparams(2),
        name="merge",
    )(*branches, z_rest, z_rest, z_rest, w_branch, w_branch, w_branch)


def _rope_tables(pos, reps):
    def tab(dim):
        half = dim // 2
        freq = ROPE_THETA ** (-jnp.arange(half, dtype=F32) / half)
        ang = pos.astype(F32)[:, None] * freq[None, :]
        cos, sin = jnp.cos(ang), jnp.sin(ang)
        n = LANES // dim
        c = jnp.tile(jnp.concatenate([cos, cos], axis=1), (reps, n))
        s = jnp.tile(jnp.concatenate([-sin, sin], axis=1), (reps, n))
        return c, s
    cq, sq = tab(HEAD_DIM)
    ci, si = tab(IDX_DIM)
    return cq, sq, ci, si


def _pick_tile(m, pref):
    t = min(m, pref)
    while m % t:
        t //= 2
    return t


def kernel(x_prompt, x_sample, mem_prompt, cache_k, cache_v, cache_idx_k, cache_mem_k, cache_mem_v, state_conv, state_h, page_table, norm_g, w_in, conv_w, conv_b, w_rgate, b_rgate, w_igate, b_igate, lru_lambda, mem_norm_g, w_mem_kv, w_branch, w_out, final_norm_g):
    bp, seq, d = x_prompt.shape
    bs, t_new, _ = x_sample.shape
    depth = w_in.shape[0]
    n_pool, page = cache_k.shape[1], cache_k.shape[2]
    n_pages = page_table.shape[1]
    past = n_pages * page
    mem_tokens = mem_prompt.shape[1]
    lru_w = conv_w.shape[2]
    mem_w = w_mem_kv.shape[2] // 2
    mem_hd = mem_w // MEM_HEADS
    rest0 = ATTN_W + 2 * KV_W + QI_W + IDX_DIM + IDX_HEADS
    n_rest = w_in.shape[2] - rest0
    topk_s = min(INDEX_TOPK, (past + t_new) // 4)
    pages_per_step = _pick_tile(n_pages, 16)

    tabs_p = _rope_tables(jnp.arange(seq), bp)
    tabs_s = _rope_tables(past + jnp.arange(t_new), bs)

    mp, ms = bp * seq, bs * t_new
    xp = x_prompt.reshape(mp, d)
    xs = x_sample.reshape(ms, d)
    mem2d = mem_prompt.reshape(bp * mem_tokens, d)
    conv0 = jnp.zeros((bp, CONV_W - 1, lru_w), F32)
    h0 = jnp.zeros((bp, 1, lru_w), F32)
    pool_k = cache_k.reshape(depth, n_pool, page * KV_HEADS, HEAD_DIM)
    pool_v = cache_v.reshape(depth, n_pool, page * KV_HEADS, HEAD_DIM)
    w_in_t = jnp.swapaxes(w_in, 1, 2)
    pool_ik_t = jnp.swapaxes(cache_idx_k, 2, 3)
    mem_k_s = cache_mem_k.reshape(depth, bs, mem_tokens, mem_w)
    mem_v_s = cache_mem_v.reshape(depth, bs, mem_tokens, mem_w)
    h_s0 = state_h.reshape(depth, bs, 1, lru_w)
    lru_w_args = (conv_w, conv_b, w_rgate, b_rgate, w_igate, b_igate, lru_lambda)

    tm_p = _pick_tile(mp, 256)
    tm_mm = _pick_tile(mp, 1024)
    tm_out = _pick_tile(mp, 512)

    st_p, st_s, mk_list, mv_list = [], [], [], []
    for l in range(depth):
        m_kv = _norm_matmul(mem2d, mem_norm_g, w_mem_kv, l, _pick_tile(bp * mem_tokens, 512), 1024)
        mk = m_kv[:, :mem_w].reshape(bp, mem_tokens, mem_w)
        mv = m_kv[:, mem_w:].reshape(bp, mem_tokens, mem_w)
        mk_list.append(mk.reshape(bp, mem_tokens, MEM_HEADS, mem_hd))
        mv_list.append(mv.reshape(bp, mem_tokens, MEM_HEADS, mem_hd))

        xn, k, v, ki, kb, kib, q_t, qi_t, wi_t, vb_t = _proj_attn(xp, norm_g, w_in_t, tabs_p, l, tm_p, True)
        z_rest = _matmul(xn, w_in_t, l, rest0, n_rest, tm_mm, 1024, True)
        br_att = _prompt_attention(qi_t, wi_t, kib, q_t, kb, vb_t, z_rest, bp, seq)
        br_lru, h_last, conv_new = _lru(z_rest, conv0, h0, *lru_w_args, l, bp, seq, _pick_tile(seq, 256), BF16)
        br_mem = _mem_attention(z_rest, mk, mv, bp, seq, _pick_tile(seq, 512), BF16)
        merged = _merge([br_att, br_lru, br_mem], z_rest, w_branch, l, tm_out, 512)
        xp = _matmul(merged, w_out, l, 0, d, tm_out, 1024, False, residual=xp)
        st_p.append((k.reshape(bp, seq, KV_HEADS, HEAD_DIM), v.reshape(bp, seq, KV_HEADS, HEAD_DIM),
                     ki.reshape(bp, seq, IDX_DIM), conv_new, h_last.reshape(bp, lru_w)))

        xn, k, v, ki, kb, kib, q, qi, wi, vb = _proj_attn(xs, norm_g, w_in_t, tabs_s, l, ms, False)
        z_rest = _matmul(xn, w_in_t, l, rest0, n_rest, ms, 1024, True)
        qs = qi.reshape(ms * IDX_HEADS, IDX_DIM)
        ws = wi.reshape(ms * IDX_HEADS, 1)
        pad_rows = lambda a: jnp.pad(a.reshape(bs, t_new, a.shape[1]), ((0, 0), (0, LANES - t_new), (0, 0)))
        sc = _sample_index(page_table, qs, ws, pool_ik_t, l, bs, t_new, _pick_tile(n_pages, 2 * pages_per_step))
        bias, bias_new = _sample_select(sc, qs, ws, pad_rows(kib), bs, t_new, topk_s)
        br_att = _sample_attention(page_table, q, bias, bias_new, pad_rows(kb), pad_rows(vb), z_rest,
                                   pool_k, pool_v, l, bs, t_new, pages_per_step)
        br_lru, h_last, conv_new = _lru(z_rest, state_conv[l], h_s0[l], *lru_w_args, l, bs, t_new, t_new, F32)
        br_mem = _mem_attention(z_rest, mem_k_s[l], mem_v_s[l], bs, t_new, t_new, F32)
        merged = _merge([br_att, br_lru, br_mem], z_rest, w_branch, l, ms, 512)
        xs = _matmul(merged, w_out, l, 0, d, ms, 1024, False, residual=xs)
        st_s.append((k.reshape(bs, t_new, KV_HEADS, HEAD_DIM), v.reshape(bs, t_new, KV_HEADS, HEAD_DIM),
                     ki.reshape(bs, t_new, IDX_DIM), conv_new, h_last.reshape(bs, lru_w)))

    g_f = final_norm_g.reshape(1, d)
    y_prompt = _rmsnorm(xp, g_f, _pick_tile(mp, 512)).reshape(bp, seq, d)
    y_sample = _rmsnorm(xs, g_f, ms).reshape(bs, t_new, d)
    k_p, v_p, ik_p, conv_p, h_p = [jnp.stack(t) for t in zip(*st_p)]
    k_s, v_s, ik_s, conv_s, h_s = [jnp.stack(t) for t in zip(*st_s)]
    return (y_prompt, y_sample, k_p, v_p, ik_p, conv_p, h_p, jnp.stack(mk_list), jnp.stack(mv_list),
            k_s, v_s, ik_s, conv_s, h_s)
```

```python
import functools

import jax
import jax.numpy as jnp
from jax import lax
from jax.experimental import pallas as pl
from jax.experimental.pallas import tpu as pltpu

F32 = jnp.float32
BF16 = jnp.bfloat16
I32 = jnp.int32

N_HEADS = 8
HEAD_DIM = 128
KV_HEADS = 2
HEADS_PER_KV = N_HEADS // KV_HEADS
IDX_HEADS = 8
IDX_DIM = 64
INDEX_TOPK = 256
LRU_BLOCKS = 8
CONV_W = 4
LRU_C = 8.0
MEM_HEADS = 4
N_BRANCH = 3
ROPE_THETA = 10000.0
NORM_EPS = 1e-6

ATTN_W = N_HEADS * HEAD_DIM
KV_W = KV_HEADS * HEAD_DIM
QI_W = IDX_HEADS * IDX_DIM

LANES = 128
SUBLANES = 8
VMEM_LIMIT = 52 * 1024 * 1024
CAST_ROWS = 256
COUNT_ROWS = 8 * SUBLANES
LOG2E = 1.4426950408889634
QK_SCALE = HEAD_DIM ** -0.5 * LOG2E

NEG_BIG = -1e30
INT_MIN = -(2 ** 31)


def _cparams(n_axes):
    return pltpu.CompilerParams(dimension_semantics=("arbitrary",) * n_axes,
                                vmem_limit_bytes=VMEM_LIMIT)


def _sigmoid(x):
    return 1.0 / (1.0 + jnp.exp(-x))


def _silu(x):
    return x * _sigmoid(x)


def _dot(a, b):
    return jnp.dot(a, b, preferred_element_type=F32)


def _dot_nt(a, b):
    return lax.dot_general(a, b, (((1,), (1,)), ((), ())), preferred_element_type=F32)


def _rms(x, g):
    ms = jnp.mean(x * x, axis=-1, keepdims=True)
    return (x * lax.rsqrt(ms + NORM_EPS)) * g


def _cast_weight(dst_ref, src_ref):
    k = dst_ref.shape[0]
    for r in range(0, k, CAST_ROWS):
        rows = slice(r, min(r + CAST_ROWS, k))
        dst_ref[rows, :] = src_ref[rows, :].astype(BF16)


def _proj_attn_kernel(x_ref, g_ref, w_ref, cq_ref, sq_ref, ci_ref, si_ref,
                      xn_ref, k_ref, v_ref, ki_ref, kb_ref, kib_ref, q_ref, qi_ref, wi_ref, vb_ref,
                      wbf_ref, *, layer, transposed):
    @pl.when(pl.program_id(0) == 0)
    def _():
        _cast_weight(wbf_ref, w_ref)

    xnb = _rms(x_ref[...], g_ref[layer:layer + 1, :]).astype(BF16)
    xn_ref[...] = xnb
    z = _dot_nt(xnb, wbf_ref[...])

    cq, sq, ci, si = cq_ref[...], sq_ref[...], ci_ref[...], si_ref[...]
    lane = lax.broadcasted_iota(I32, cq.shape, 1)
    first_half = (lane & (IDX_DIM // 2)) == 0

    def rope128(t):
        return t * cq + pltpu.roll(t, HEAD_DIM // 2, 1) * sq

    def rope64(t):
        rot = jnp.where(first_half, pltpu.roll(t, LANES - IDX_DIM // 2, 1),
                        pltpu.roll(t, IDX_DIM // 2, 1))
        return t * ci + rot * si

    def tile(c):
        return z[:, c * LANES:(c + 1) * LANES]

    for h in range(N_HEADS):
        qh = rope128(tile(h)) * QK_SCALE
        if transposed:
            q_ref[h * HEAD_DIM:(h + 1) * HEAD_DIM, :] = qh.T.astype(BF16)
        else:
            q_ref[:, h * HEAD_DIM:(h + 1) * HEAD_DIM] = qh
    c0 = ATTN_W // LANES
    for h in range(KV_HEADS):
        sl = slice(h * HEAD_DIM, (h + 1) * HEAD_DIM)
        kr = rope128(tile(c0 + h))
        k_ref[:, sl] = kr
        kb_ref[:, sl] = kr.astype(BF16)
    c0 += KV_W // LANES
    for h in range(KV_HEADS):
        sl = slice(h * HEAD_DIM, (h + 1) * HEAD_DIM)
        vh = tile(c0 + h)
        v_ref[:, sl] = vh
        if transposed:
            vb_ref[sl, :] = vh.T.astype(BF16)
        else:
            vb_ref[:, sl] = vh.astype(BF16)
    c0 += KV_W // LANES
    for c in range(QI_W // LANES):
        qc = rope64(tile(c0 + c))
        if transposed:
            qi_ref[c * LANES:(c + 1) * LANES, :] = qc.T.astype(BF16)
        else:
            qi_ref[:, c * LANES:(c + 1) * LANES] = qc.astype(BF16)
    c0 += QI_W // LANES
    last = tile(c0)
    kir = rope64(last)[:, :IDX_DIM]
    ki_ref[...] = kir
    kib_ref[...] = kir.astype(BF16)
    if transposed:
        wi_ref[...] = last.T[IDX_DIM:IDX_DIM + IDX_HEADS, :] * (IDX_HEADS ** -0.5)
    else:
        wi_ref[...] = last[:, IDX_DIM:IDX_DIM + IDX_HEADS] * (IDX_HEADS ** -0.5)


def _proj_attn(x, norm_g, w_in_t, tabs, layer, tm, transposed):
    m, d = x.shape
    na = ATTN_W + 2 * KV_W + QI_W + LANES
    row = lambda i: (i, 0)
    col = lambda i: (0, i)
    const = lambda i: (0, 0)

    def tok(width, dtype):
        return ((m, width), dtype, pl.BlockSpec((tm, width), row))

    def feat(width, dtype):
        return ((width, m), dtype, pl.BlockSpec((width, tm), col))

    outs = [tok(d, BF16), tok(KV_W, F32), tok(KV_W, F32), tok(IDX_DIM, F32), tok(KV_W, BF16), tok(IDX_DIM, BF16)]
    if transposed:
        outs += [feat(ATTN_W, BF16), feat(QI_W, BF16), feat(IDX_HEADS, F32), feat(KV_W, BF16)]
    else:
        outs += [tok(ATTN_W, F32), tok(QI_W, BF16), tok(IDX_HEADS, F32), tok(KV_W, BF16)]
    return pl.pallas_call(
        functools.partial(_proj_attn_kernel, layer=layer, transposed=transposed),
        grid=(m // tm,),
        in_specs=[pl.BlockSpec((tm, d), row), pl.BlockSpec(norm_g.shape, const),
                  pl.BlockSpec((None, na, d), lambda i: (layer, 0, 0), pipeline_mode=pl.Buffered(1))]
                 + [pl.BlockSpec((tm, LANES), row)] * 4,
        out_specs=[o[2] for o in outs],
        out_shape=[jax.ShapeDtypeStruct(o[0], o[1]) for o in outs],
        scratch_shapes=[pltpu.VMEM((na, d), BF16)],
        compiler_params=_cparams(1),
        name="proj_attn",
    )(x, norm_g, w_in_t, *tabs)


def _mm_kernel(*refs, w_transposed, residual):
    x_ref, w_ref = refs[0], refs[1]
    r_ref = refs[2] if residual else None
    o_ref, wbf_ref = refs[-2], refs[-1]

    @pl.when(pl.program_id(1) == 0)
    def _():
        _cast_weight(wbf_ref, w_ref.at[0] if w_transposed else w_ref)

    out = _dot_nt(x_ref[...], wbf_ref[...]) if w_transposed else _dot(x_ref[...], wbf_ref[...])
    if residual:
        out = r_ref[...] + out
    o_ref[...] = out


def _matmul(x, w, layer, col0, n, tm, tn, w_transposed, residual=None):
    m, k = x.shape
    assert n % tn == 0 and m % tm == 0
    if w_transposed:
        assert col0 % SUBLANES == 0
        w_spec = pl.BlockSpec((pl.Element(1), pl.Element(tn), pl.Element(k)),
                              lambda j, i: (layer, (col0 // SUBLANES + j * (tn // SUBLANES)) * SUBLANES, 0))
        w_scratch = pltpu.VMEM((tn, k), BF16)
    else:
        assert col0 % tn == 0
        w_spec = pl.BlockSpec((None, k, tn), lambda j, i: (layer, 0, col0 // tn + j))
        w_scratch = pltpu.VMEM((k, tn), BF16)
    in_specs = [pl.BlockSpec((tm, k), lambda j, i: (i, 0)), w_spec]
    args = [x, w]
    if residual is not None:
        in_specs.append(pl.BlockSpec((tm, tn), lambda j, i: (i, j)))
        args.append(residual)
    return pl.pallas_call(
        functools.partial(_mm_kernel, w_transposed=w_transposed, residual=residual is not None),
        grid=(n // tn, m // tm),
        in_specs=in_specs,
        out_specs=pl.BlockSpec((tm, tn), lambda j, i: (i, j)),
        out_shape=jax.ShapeDtypeStruct((m, n), F32),
        scratch_shapes=[w_scratch],
        compiler_params=_cparams(2),
        name="matmul_residual" if residual is not None else "matmul",
    )(*args)


def _norm_mm_kernel(x_ref, g_ref, w_ref, o_ref, *, layer):
    xn = _rms(x_ref[...], g_ref[layer:layer + 1, :]).astype(BF16)
    o_ref[...] = _dot(xn, w_ref[...].astype(BF16))


def _norm_matmul(x, g, w, layer, tm, tn):
    m, k = x.shape
    n = w.shape[2]
    return pl.pallas_call(
        functools.partial(_norm_mm_kernel, layer=layer),
        grid=(n // tn, m // tm),
        in_specs=[pl.BlockSpec((tm, k), lambda j, i: (i, 0)), pl.BlockSpec(g.shape, lambda j, i: (0, 0)),
                  pl.BlockSpec((None, k, tn), lambda j, i: (layer, 0, j))],
        out_specs=pl.BlockSpec((tm, tn), lambda j, i: (i, j)),
        out_shape=jax.ShapeDtypeStruct((m, n), F32),
        compiler_params=_cparams(2),
        name="norm_matmul",
    )(x, g, w)


def _rmsnorm_kernel(x_ref, g_ref, o_ref):
    o_ref[...] = _rms(x_ref[...], g_ref[...])


def _rmsnorm(x, g, tm):
    m, d = x.shape
    return pl.pallas_call(
        _rmsnorm_kernel,
        grid=(m // tm,),
        in_specs=[pl.BlockSpec((tm, d), lambda i: (i, 0)), pl.BlockSpec((1, d), lambda i: (0, 0))],
        out_specs=pl.BlockSpec((tm, d), lambda i: (i, 0)),
        out_shape=jax.ShapeDtypeStruct((m, d), F32),
        compiler_params=_cparams(1),
        name="final_rmsnorm",
    )(x, g)


def _sortable_key(score):
    bits = pltpu.bitcast(score, I32)
    return bits ^ ((bits >> 31) & jnp.int32(0x7FFFFFFF))


def _kth_largest_key(count_ge, shape, topk):
    zero = jnp.zeros(shape, I32)
    ans = jnp.where(count_ge(zero) >= topk, zero, jnp.full(shape, INT_MIN, I32))

    def bit_body(j, ans):
        cand = ans | lax.shift_left(jnp.int32(1), 30 - j)
        return jnp.where(count_ge(cand) >= topk, cand, ans)

    return lax.fori_loop(0, 31, bit_body, ans)


def _tie_cutoff(count_tie_lt, need, shape, idx_bits):
    def bit_body(j, cut):
        cand = cut | lax.shift_left(jnp.int32(1), idx_bits - 1 - j)
        return jnp.where(count_tie_lt(cand) < need, cand, cut)

    return lax.fori_loop(0, idx_bits, bit_body, jnp.zeros(shape, I32))


def _prompt_attn_kernel(qi_ref, wi_ref, kib_ref, q_ref, kb_ref, vb_ref, ga_ref, o_ref,
                        key_ref, bias_ref, acc_ref, *, tq, kc, topk, seq):
    i = pl.program_id(1)
    nch = ((i + 1) * tq + kc - 1) // kc
    key_pos = lax.broadcasted_iota(I32, (kc, tq), 0)
    q_pos = i * tq + lax.broadcasted_iota(I32, (kc, tq), 1)
    idx_bits = max(1, (seq - 1).bit_length())

    def chunk(ref, c):
        return ref[pl.ds(pl.multiple_of(c * kc, kc), kc), :]

    wi = wi_ref[...] * (IDX_DIM ** -0.5)

    def score_body(c, carry):
        kic = chunk(kib_ref, c)
        acc = jnp.zeros((kc, tq), F32)
        for h in range(IDX_HEADS):
            s = _dot(kic, qi_ref[h * IDX_DIM:(h + 1) * IDX_DIM, :])
            acc = acc + jnp.maximum(s, 0.0) * wi[h:h + 1, :]
        key = jnp.where(c * kc + key_pos <= q_pos, _sortable_key(acc), INT_MIN)
        key_ref[pl.ds(pl.multiple_of(c * kc, kc), kc), :] = key
        return carry

    lax.fori_loop(0, nch, score_body, 0)

    need_search = (i + 1) * tq > topk
    pos_tile = key_pos[:, :LANES]

    def select_keys(sl):
        def chunk_count(pred):
            def body(c, acc):
                k = key_ref[pl.ds(pl.multiple_of(c * kc, kc), kc), sl]
                ind = jnp.where(pred(k, c * kc + pos_tile), 1.0, 0.0)
                return acc + jnp.sum(ind.reshape(kc // COUNT_ROWS, COUNT_ROWS, LANES), axis=0)
            acc = lax.fori_loop(0, nch, body, jnp.zeros((COUNT_ROWS, LANES), F32))
            return jnp.sum(acc, axis=0, keepdims=True)

        def count_ge(cand):
            return chunk_count(lambda k, pos: k >= cand)

        def write_bias(keep):
            def body(c, carry):
                rows = pl.ds(pl.multiple_of(c * kc, kc), kc)
                bias_ref[rows, sl] = jnp.where(keep(key_ref[rows, sl], c * kc + pos_tile), 0.0, NEG_BIG)
                return carry
            lax.fori_loop(0, nch, body, 0)

        tile1 = (1, LANES)
        thr = lax.cond(need_search,
                       lambda: jnp.maximum(_kth_largest_key(count_ge, tile1, topk), INT_MIN + 1),
                       lambda: jnp.full(tile1, INT_MIN + 1, I32))
        write_bias(lambda k, pos: k >= thr)

        @pl.when(jnp.max(count_ge(thr)) > topk)
        def _():
            need = topk - count_ge(thr + 1)
            count_tie_lt = lambda cut: chunk_count(lambda k, pos: jnp.where(k == thr, pos, seq) < cut)
            cut = _tie_cutoff(count_tie_lt, need, tile1, idx_bits)
            write_bias(lambda k, pos: (k > thr) | ((k == thr) & (pos <= cut)))

    for lt in range(tq // LANES):
        select_keys(slice(lt * LANES, (lt + 1) * LANES))

    qw = HEADS_PER_KV * tq
    q4 = [jnp.concatenate([q_ref[(g * HEADS_PER_KV + hh) * HEAD_DIM:(g * HEADS_PER_KV + hh + 1) * HEAD_DIM, :]
                           for hh in range(HEADS_PER_KV)], axis=1) for g in range(KV_HEADS)]
    acc_ref[...] = jnp.zeros(acc_ref.shape, F32)

    def attn_body(c, carry):
        off = pl.multiple_of(c * kc, kc)
        bias = chunk(bias_ref, c)
        bias4 = jnp.concatenate([bias] * HEADS_PER_KV, axis=1)
        out = []
        for g in range(KV_HEADS):
            m_prev, l_prev = carry[g]
            hs = slice(g * HEAD_DIM, (g + 1) * HEAD_DIM)
            s = _dot(kb_ref[pl.ds(off, kc), hs], q4[g]) + bias4
            m_next = jnp.maximum(m_prev, jnp.max(s, axis=0, keepdims=True))
            p = jnp.exp2(s - m_next)
            alpha = jnp.exp2(m_prev - m_next)
            l_next = alpha * l_prev + jnp.sum(p, axis=0, keepdims=True)
            acc_ref[g] = acc_ref[g] * alpha + _dot(vb_ref[hs, pl.ds(off, kc)], p.astype(BF16))
            out.append((m_next, l_next))
        return tuple(out)

    init = tuple((jnp.full((1, qw), NEG_BIG, F32), jnp.zeros((1, qw), F32)) for _ in range(KV_HEADS))
    stats = lax.fori_loop(0, nch, attn_body, init)
    for g in range(KV_HEADS):
        o_t = acc_ref[g] / stats[g][1]
        for hh in range(HEADS_PER_KV):
            sl = slice((g * HEADS_PER_KV + hh) * HEAD_DIM, (g * HEADS_PER_KV + hh + 1) * HEAD_DIM)
            o_ref[:, sl] = (o_t[:, hh * tq:(hh + 1) * tq].T * _silu(ga_ref[:, sl])).astype(BF16)


def _prompt_attention(qi_t, wi_t, kib, q_t, kb, vb_t, z_rest, batch, seq):
    tq = _pick_tile(seq, 2 * LANES)
    nq = seq // tq
    kc = _pick_tile(seq, 512)
    topk = min(INDEX_TOPK, seq // 4)
    blk_t = lambda b, i: (0, b * nq + i)
    blk = lambda b, i: (b * nq + i, 0)
    return pl.pallas_call(
        functools.partial(_prompt_attn_kernel, tq=tq, kc=kc, topk=topk, seq=seq),
        grid=(batch, nq),
        in_specs=[pl.BlockSpec((QI_W, tq), blk_t), pl.BlockSpec((IDX_HEADS, tq), blk_t),
                  pl.BlockSpec((seq, IDX_DIM), lambda b, i: (b, 0)), pl.BlockSpec((ATTN_W, tq), blk_t),
                  pl.BlockSpec((seq, KV_W), lambda b, i: (b, 0)), pl.BlockSpec((KV_W, seq), lambda b, i: (0, b)),
                  pl.BlockSpec((tq, ATTN_W), blk)],
        out_specs=pl.BlockSpec((tq, ATTN_W), blk),
        out_shape=jax.ShapeDtypeStruct((batch * seq, ATTN_W), BF16),
        scratch_shapes=[pltpu.VMEM((seq, tq), I32), pltpu.VMEM((seq, tq), F32),
                        pltpu.VMEM((KV_HEADS, HEAD_DIM, HEADS_PER_KV * tq), F32)],
        compiler_params=_cparams(2),
        name="prompt_attention",
    )(qi_t, wi_t, kib, q_t, kb, vb_t, z_rest)


def _sample_index_kernel(pt_ref, qs_ref, ws_ref, *rest, pages_per_step, t_new):
    page_refs, o_ref = rest[:pages_per_step], rest[pages_per_step]
    qs = qs_ref[...]
    ws = ws_ref[...] * (IDX_DIM ** -0.5)
    page = page_refs[0].shape[1]
    wsb = jnp.broadcast_to(ws, (t_new * IDX_HEADS, page))
    for j in range(pages_per_step):
        kp = page_refs[j][...].astype(BF16)
        s = jnp.maximum(_dot(qs, kp), 0.0) * wsb
        o_ref[:, j * page:(j + 1) * page] = jnp.sum(s.reshape(t_new, IDX_HEADS, page), axis=1)


def _sample_index(page_table, qs, ws, pool_ik_t, layer, dec_batch, t_new, pages_per_step):
    n_pages = page_table.shape[1]
    page = pool_ik_t.shape[3]
    steps = n_pages // pages_per_step
    rows = t_new * IDX_HEADS

    def page_map(j):
        return lambda b, s, pt: (layer, pt[b, s * pages_per_step + j], 0, 0)

    grid_spec = pltpu.PrefetchScalarGridSpec(
        num_scalar_prefetch=1,
        grid=(dec_batch, steps),
        in_specs=[pl.BlockSpec((rows, IDX_DIM), lambda b, s, pt: (b, 0)),
                  pl.BlockSpec((rows, 1), lambda b, s, pt: (b, 0))]
                 + [pl.BlockSpec((None, None, IDX_DIM, page), page_map(j)) for j in range(pages_per_step)],
        out_specs=pl.BlockSpec((t_new, pages_per_step * page), lambda b, s, pt: (b, s)),
    )
    return pl.pallas_call(
        functools.partial(_sample_index_kernel, pages_per_step=pages_per_step, t_new=t_new),
        grid_spec=grid_spec,
        out_shape=jax.ShapeDtypeStruct((dec_batch * t_new, n_pages * page), F32),
        compiler_params=_cparams(2),
        name="sample_index",
    )(page_table, qs, ws, *([pool_ik_t] * pages_per_step))


def _sample_select_kernel(sc_ref, qs_ref, ws_ref, kn_ref, bias_ref, bias_new_ref, key_ref, keyn_ref,
                          *, dec_batch, t_new, past, topk):
    rows = dec_batch * t_new
    hrows = t_new * IDX_HEADS
    ws = ws_ref[...] * (IDX_DIM ** -0.5)
    lane_n = lax.broadcasted_iota(I32, (rows, LANES), 1)
    row_n = lax.broadcasted_iota(I32, (rows, LANES), 0) % t_new
    s_new = []
    for b in range(dec_batch):
        hs = slice(b * hrows, (b + 1) * hrows)
        s = jnp.maximum(_dot_nt(qs_ref[hs, :], kn_ref[b]), 0.0) * jnp.broadcast_to(ws[hs, :], (hrows, LANES))
        s_new.append(jnp.sum(s.reshape(t_new, IDX_HEADS, LANES), axis=1))
    s_new = jnp.concatenate(s_new, axis=0) + 0.0
    keyn_ref[...] = jnp.where(lane_n <= row_n, _sortable_key(s_new), INT_MIN)
    key_ref[...] = _sortable_key(sc_ref[...] + 0.0)
    pos = lax.broadcasted_iota(I32, (rows, past), 1)
    idx_bits = (past + LANES - 1).bit_length()
    col1 = (rows, 1)

    def count(pred):
        a = jnp.sum(jnp.where(pred(key_ref[...], pos), 1.0, 0.0), axis=1, keepdims=True)
        b = jnp.sum(jnp.where(pred(keyn_ref[...], past + lane_n), 1.0, 0.0), axis=1, keepdims=True)
        return a + b

    count_ge = lambda cand: count(lambda k, p: k >= cand)
    thr = jnp.maximum(_kth_largest_key(count_ge, col1, topk), INT_MIN + 1)
    bias_ref[...] = jnp.where(key_ref[...] >= thr, 0.0, NEG_BIG)
    bias_new_ref[...] = jnp.where(keyn_ref[...] >= thr, 0.0, NEG_BIG)

    @pl.when(jnp.max(count_ge(thr)) > topk)
    def _():
        need = topk - count_ge(thr + 1)
        big = jnp.int32(2 ** idx_bits)
        count_tie_lt = lambda cut: count(lambda k, p: jnp.where(k == thr, p, big) < cut)
        cut = _tie_cutoff(count_tie_lt, need, col1, idx_bits)
        keep = lambda k, p: (k > thr) | ((k == thr) & (p <= cut))
        bias_ref[...] = jnp.where(keep(key_ref[...], pos), 0.0, NEG_BIG)
        bias_new_ref[...] = jnp.where(keep(keyn_ref[...], past + lane_n), 0.0, NEG_BIG)


def _sample_select(sc, qs, ws, knew_pad, dec_batch, t_new, topk):
    rows, past = sc.shape
    whole = lambda a: pl.BlockSpec(a.shape, lambda i: (0,) * a.ndim)
    out_shape = [jax.ShapeDtypeStruct((rows, past), F32), jax.ShapeDtypeStruct((rows, LANES), F32)]
    return pl.pallas_call(
        functools.partial(_sample_select_kernel, dec_batch=dec_batch, t_new=t_new, past=past, topk=topk),
        grid=(1,),
        in_specs=[whole(sc), whole(qs), whole(ws), whole(knew_pad)],
        out_specs=[whole(o) for o in out_shape],
        out_shape=out_shape,
        scratch_shapes=[pltpu.VMEM((rows, past), I32), pltpu.VMEM((rows, LANES), I32)],
        compiler_params=_cparams(1),
        name="sample_select",
    )(sc, qs, ws, knew_pad)


def _sample_attn_kernel(pt_ref, q_ref, bias_ref, biasn_ref, kn_ref, vn_ref, ga_ref, *rest,
                        pages_per_step, t_new):
    k_refs = rest[:pages_per_step]
    v_refs = rest[pages_per_step:2 * pages_per_step]
    o_ref, m_ref, l_ref, acc_ref = rest[2 * pages_per_step:]
    step = pl.program_id(1)
    page = k_refs[0].shape[0] // KV_HEADS

    @pl.when(step == 0)
    def _():
        m_ref[...] = jnp.full(m_ref.shape, NEG_BIG, F32)
        l_ref[...] = jnp.zeros(l_ref.shape, F32)
        acc_ref[...] = jnp.zeros(acc_ref.shape, F32)

    def q_group(g):
        return jnp.concatenate(
            [q_ref[:, (g * HEADS_PER_KV + hh) * HEAD_DIM:(g * HEADS_PER_KV + hh + 1) * HEAD_DIM]
             for hh in range(HEADS_PER_KV)], axis=0).astype(BF16)

    def head_rows(ref, g):
        return ref[pl.ds(g, page, stride=KV_HEADS), :].astype(BF16)

    def update(g, s, v_of):
        m_prev = m_ref[g]
        m_next = jnp.maximum(m_prev, jnp.max(s, axis=1, keepdims=True))
        n = s.shape[1] // LANES
        p = jnp.exp2(s - jnp.concatenate([m_next] * n, axis=1))
        alpha = jnp.exp2(m_prev - m_next)
        l_ref[g] = alpha * l_ref[g] + jnp.sum(p, axis=1, keepdims=True)
        pv = _dot(p[:, :page].astype(BF16), v_of(0))
        for j in range(1, s.shape[1] // page):
            pv = pv + _dot(p[:, j * page:(j + 1) * page].astype(BF16), v_of(j))
        acc_ref[g] = acc_ref[g] * alpha + pv
        m_ref[g] = m_next

    bias4 = jnp.concatenate([bias_ref[...]] * HEADS_PER_KV, axis=0)
    for g in range(KV_HEADS):
        qg = q_group(g)
        s = jnp.concatenate([_dot_nt(qg, head_rows(k_refs[j], g)) for j in range(pages_per_step)], axis=1)
        update(g, s + bias4, lambda j: head_rows(v_refs[j], g))

    @pl.when(step == pl.num_programs(1) - 1)
    def _():
        biasn4 = jnp.concatenate([biasn_ref[...]] * HEADS_PER_KV, axis=0)
        for g in range(KV_HEADS):
            hs = slice(g * HEAD_DIM, (g + 1) * HEAD_DIM)
            s = _dot_nt(q_group(g), kn_ref[:, hs]) + biasn4
            update(g, s, lambda j: vn_ref[:, hs])
            o = acc_ref[g] / l_ref[g]
            for hh in range(HEADS_PER_KV):
                sl = slice((g * HEADS_PER_KV + hh) * HEAD_DIM, (g * HEADS_PER_KV + hh + 1) * HEAD_DIM)
                o_ref[:, sl] = o[hh * t_new:(hh + 1) * t_new] * _silu(ga_ref[:, sl])


def _sample_attention(page_table, q, bias, bias_new, knew_pad, vnew_pad, z_rest, pool_k, pool_v, layer,
                      dec_batch, t_new, pages_per_step):
    n_pages = page_table.shape[1]
    page_rows = pool_k.shape[2]
    page = page_rows // KV_HEADS
    steps = n_pages // pages_per_step
    rows = HEADS_PER_KV * t_new
    per_b = lambda b, s, pt: (b, 0)

    def page_map(j):
        return lambda b, s, pt: (layer, pt[b, s * pages_per_step + j], 0, 0)

    page_specs = [pl.BlockSpec((None, None, page_rows, HEAD_DIM), page_map(j)) for j in range(pages_per_step)]
    grid_spec = pltpu.PrefetchScalarGridSpec(
        num_scalar_prefetch=1,
        grid=(dec_batch, steps),
        in_specs=[pl.BlockSpec((t_new, ATTN_W), per_b),
                  pl.BlockSpec((t_new, pages_per_step * page), lambda b, s, pt: (b, s)),
                  pl.BlockSpec((t_new, LANES), per_b),
                  pl.BlockSpec((None, LANES, KV_W), lambda b, s, pt: (b, 0, 0)),
                  pl.BlockSpec((None, LANES, KV_W), lambda b, s, pt: (b, 0, 0)),
                  pl.BlockSpec((t_new, ATTN_W), per_b)] + page_specs + page_specs,
        out_specs=pl.BlockSpec((t_new, ATTN_W), per_b),
        scratch_shapes=[pltpu.VMEM((KV_HEADS, rows, LANES), F32), pltpu.VMEM((KV_HEADS, rows, LANES), F32),
                        pltpu.VMEM((KV_HEADS, rows, HEAD_DIM), F32)],
    )
    return pl.pallas_call(
        functools.partial(_sample_attn_kernel, pages_per_step=pages_per_step, t_new=t_new),
        grid_spec=grid_spec,
        out_shape=jax.ShapeDtypeStruct((dec_batch * t_new, ATTN_W), F32),
        compiler_params=_cparams(2),
        name="sample_attention",
    )(page_table, q, bias, bias_new, knew_pad, vnew_pad, z_rest,
      *([pool_k] * pages_per_step), *([pool_v] * pages_per_step))


def _lru_kernel(xr_ref, gr_ref, cp_ref, hp_ref, cw_ref, cb_ref, wr_ref, br_ref, wg_ref, bg_ref, lam_ref,
                o_ref, hl_ref, cn_ref, ext_ref, hc_ref, a_ref, b_ref, *, tt, layer):
    t = pl.program_id(1)
    tail = CONV_W - 1
    base = SUBLANES
    width = xr_ref.shape[1]
    bw = width // LRU_BLOCKS
    vec = lambda ref: ref[layer:layer + 1, :]

    @pl.when(t == 0)
    def _():
        ext_ref[base - tail:base, :] = cp_ref[...]
        hc_ref[...] = hp_ref[...]

    x = xr_ref[...]
    ext_ref[base:base + tt, :] = x
    cw = cw_ref[...]
    xc = vec(cb_ref) + x * cw[tail:tail + 1, :]
    for j in range(tail):
        xc = xc + ext_ref[base - tail + j: base - tail + j + tt, :] * cw[j:j + 1, :]
    new_tail = x[tt - tail:, :]
    ext_ref[base - tail:base, :] = new_tail

    xcb = xc.astype(BF16)
    r_lin = jnp.concatenate([_dot(xcb[:, n * bw:(n + 1) * bw], wr_ref[n].astype(BF16))
                             for n in range(LRU_BLOCKS)], axis=1)
    g_lin = jnp.concatenate([_dot(xcb[:, n * bw:(n + 1) * bw], wg_ref[n].astype(BF16))
                             for n in range(LRU_BLOCKS)], axis=1)
    r = _sigmoid(r_lin + vec(br_ref))
    gi = _sigmoid(g_lin + vec(bg_ref))
    nl = -vec(lam_ref)
    softplus = jnp.maximum(nl, 0.0) + jnp.log1p(jnp.exp(-jnp.abs(nl)))
    log_a = (-LRU_C) * r * softplus
    a = jnp.exp(log_a)
    b = jnp.sqrt(jnp.tanh(-log_a) * (a * a + 1.0)) * (gi * xc)

    row = lax.broadcasted_iota(I32, (tt, width), 0) & (SUBLANES - 1)
    d = 1
    while d < SUBLANES:
        keep = row >= d
        a_s = jnp.where(keep, pltpu.roll(a, d, 0), 1.0)
        b_s = jnp.where(keep, pltpu.roll(b, d, 0), 0.0)
        b = a * b_s + b
        a = a * a_s
        d *= 2
    a_ref[...] = a
    b_ref[...] = b

    def group_body(gidx, h):
        off = pl.multiple_of(gidx * SUBLANES, SUBLANES)
        h8 = a_ref[pl.ds(off, SUBLANES), :] * h + b_ref[pl.ds(off, SUBLANES), :]
        b_ref[pl.ds(off, SUBLANES), :] = h8
        return h8[SUBLANES - 1:SUBLANES, :]

    h_last = lax.fori_loop(0, tt // SUBLANES, group_body, hc_ref[...])
    hc_ref[...] = h_last
    o_ref[...] = (b_ref[...] * _silu(gr_ref[...])).astype(o_ref.dtype)

    @pl.when(t == pl.num_programs(1) - 1)
    def _():
        hl_ref[...] = h_last
        cn_ref[...] = new_tail


def _lru(z_rest, conv_prev, h_prev, conv_w, conv_b, w_r, b_r, w_g, b_g, lam, layer, batch, t_len, tt, out_dtype):
    width = conv_w.shape[2]
    nt = t_len // tt
    tail = CONV_W - 1
    bw = width // LRU_BLOCKS
    whole = lambda a: pl.BlockSpec(a.shape, lambda b, t: (0,) * a.ndim)
    per_b = lambda b, t: (b, 0, 0)
    return pl.pallas_call(
        functools.partial(_lru_kernel, tt=tt, layer=layer),
        grid=(batch, nt),
        in_specs=[pl.BlockSpec((tt, width), lambda b, t: (b * nt + t, 1)),
                  pl.BlockSpec((tt, width), lambda b, t: (b * nt + t, 2)),
                  pl.BlockSpec((None, tail, width), per_b), pl.BlockSpec((None, 1, width), per_b),
                  pl.BlockSpec((None, CONV_W, width), lambda b, t: (layer, 0, 0)), whole(conv_b),
                  pl.BlockSpec((None, LRU_BLOCKS, bw, bw), lambda b, t: (layer, 0, 0, 0)), whole(b_r),
                  pl.BlockSpec((None, LRU_BLOCKS, bw, bw), lambda b, t: (layer, 0, 0, 0)), whole(b_g),
                  whole(lam)],
        out_specs=[pl.BlockSpec((tt, width), lambda b, t: (b * nt + t, 0)),
                   pl.BlockSpec((None, 1, width), per_b), pl.BlockSpec((None, tail, width), per_b)],
        out_shape=[jax.ShapeDtypeStruct((batch * t_len, width), out_dtype),
                   jax.ShapeDtypeStruct((batch, 1, width), F32),
                   jax.ShapeDtypeStruct((batch, tail, width), F32)],
        scratch_shapes=[pltpu.VMEM((SUBLANES + tt, width), F32), pltpu.VMEM((1, width), F32),
                        pltpu.VMEM((tt, width), F32), pltpu.VMEM((tt, width), F32)],
        compiler_params=_cparams(2),
        name="rglru",
    )(z_rest, z_rest, conv_prev, h_prev, conv_w, conv_b, w_r, b_r, w_g, b_g, lam)


def _mem_attn_kernel(qm_ref, gm_ref, mk_ref, mv_ref, o_ref):
    width = qm_ref.shape[1]
    hd = width // MEM_HEADS
    scale = hd ** -0.5
    for h in range(MEM_HEADS):
        sl = slice(h * hd, (h + 1) * hd)
        s = _dot_nt(qm_ref[:, sl].astype(BF16), mk_ref[:, sl].astype(BF16)) * scale
        p = jnp.exp(s - jnp.max(s, axis=1, keepdims=True))
        l = jnp.sum(p, axis=1, keepdims=True)
        o = _dot(p.astype(BF16), mv_ref[:, sl].astype(BF16)) / l
        o_ref[:, sl] = (o * _silu(gm_ref[:, sl])).astype(o_ref.dtype)


def _mem_attention(z_rest, mk, mv, batch, t_len, tm, out_dtype):
    width = mk.shape[2]
    mem = mk.shape[1]
    nt = t_len // tm
    return pl.pallas_call(
        _mem_attn_kernel,
        grid=(batch, nt),
        in_specs=[pl.BlockSpec((tm, width), lambda b, t: (b * nt + t, 3)),
                  pl.BlockSpec((tm, width), lambda b, t: (b * nt + t, 4)),
                  pl.BlockSpec((None, mem, width), lambda b, t: (b, 0, 0)),
                  pl.BlockSpec((None, mem, width), lambda b, t: (b, 0, 0))],
        out_specs=pl.BlockSpec((tm, width), lambda b, t: (b * nt + t, 0)),
        out_shape=jax.ShapeDtypeStruct((batch * t_len, width), out_dtype),
        compiler_params=_cparams(2),
        name="mem_attention",
    )(z_rest, z_rest, mk, mv)


def _merge_kernel(ba_ref, bl_ref, bm_ref, ga_ref, gl_ref, gm_ref, wa_ref, wl_ref, wm_ref, o_ref, wbf_ref):
    @pl.when(pl.program_id(1) == 0)
    def _():
        for b, w_ref in enumerate((wa_ref, wl_ref, wm_ref)):
            _cast_weight(wbf_ref.at[b], w_ref)

    out = _sigmoid(ga_ref[...]) * _dot(ba_ref[...].astype(BF16), wbf_ref[0])
    out = out + _sigmoid(gl_ref[...]) * _dot(bl_ref[...].astype(BF16), wbf_ref[1])
    out = out + _sigmoid(gm_ref[...]) * _dot(bm_ref[...].astype(BF16), wbf_ref[2])
    o_ref[...] = out.astype(BF16)


def _merge(branches, z_rest, w_branch, layer, tm, tn):
    m, bw = branches[0].shape
    d = w_branch.shape[3]
    gate_col0 = 5 * bw // tn
    br_spec = pl.BlockSpec((tm, bw), lambda j, i: (i, 0))

    def gate_spec(b):
        return pl.BlockSpec((tm, tn), lambda j, i: (i, gate_col0 + b * (d // tn) + j))

    def w_spec(b):
        return pl.BlockSpec((None, None, bw, tn), lambda j, i: (layer, b, 0, j))

    return pl.pallas_call(
        _merge_kernel,
        grid=(d // tn, m // tm),
        in_specs=[br_spec] * N_BRANCH + [gate_spec(b) for b in range(N_BRANCH)]
                 + [w_spec(b) for b in range(N_BRANCH)],
        out_specs=pl.BlockSpec((tm, tn), lambda j, i: (i, j)),
        out_shape=jax.ShapeDtypeStruct((m, d), BF16),
        scratch_shapes=[pltpu.VMEM((N_BRANCH, bw, tn), BF16)],
        compiler_params=_cparams(2),
        name="merge",
    )(*branches, z_rest, z_rest, z_rest, w_branch, w_branch, w_branch)


def _rope_tables(pos, reps):
    def tab(dim):
        half = dim // 2
        freq = ROPE_THETA ** (-jnp.arange(half, dtype=F32) / half)
        ang = pos.astype(F32)[:, None] * freq[None, :]
        cos, sin = jnp.cos(ang), jnp.sin(ang)
        n = LANES // dim
        c = jnp.tile(jnp.concatenate([cos, cos], axis=1), (reps, n))
        s = jnp.tile(jnp.concatenate([-sin, sin], axis=1), (reps, n))
        return c, s
    cq, sq = tab(HEAD_DIM)
    ci, si = tab(IDX_DIM)
    return cq, sq, ci, si


def _pick_tile(m, pref):
    t = min(m, pref)
    while m % t:
        t //= 2
    return t


def kernel(x_prompt, x_sample, mem_prompt, cache_k, cache_v, cache_idx_k, cache_mem_k, cache_mem_v, state_conv, state_h, page_table, norm_g, w_in, conv_w, conv_b, w_rgate, b_rgate, w_igate, b_igate, lru_lambda, mem_norm_g, w_mem_kv, w_branch, w_out, final_norm_g):
    bp, seq, d = x_prompt.shape
    bs, t_new, _ = x_sample.shape
    depth = w_in.shape[0]
    n_pool, page = cache_k.shape[1], cache_k.shape[2]
    n_pages = page_table.shape[1]
    past = n_pages * page
    mem_tokens = mem_prompt.shape[1]
    lru_w = conv_w.shape[2]
    mem_w = w_mem_kv.shape[2] // 2
    mem_hd = mem_w // MEM_HEADS
    rest0 = ATTN_W + 2 * KV_W + QI_W + IDX_DIM + IDX_HEADS
    n_rest = w_in.shape[2] - rest0
    topk_s = min(INDEX_TOPK, (past + t_new) // 4)
    pages_per_step = _pick_tile(n_pages, 16)

    tabs_p = _rope_tables(jnp.arange(seq), bp)
    tabs_s = _rope_tables(past + jnp.arange(t_new), bs)

    mp, ms = bp * seq, bs * t_new
    xp = x_prompt.reshape(mp, d)
    xs = x_sample.reshape(ms, d)
    mem2d = mem_prompt.reshape(bp * mem_tokens, d)
    conv0 = jnp.zeros((bp, CONV_W - 1, lru_w), F32)
    h0 = jnp.zeros((bp, 1, lru_w), F32)
    pool_k = cache_k.reshape(depth, n_pool, page * KV_HEADS, HEAD_DIM)
    pool_v = cache_v.reshape(depth, n_pool, page * KV_HEADS, HEAD_DIM)
    w_in_t = jnp.swapaxes(w_in, 1, 2)
    pool_ik_t = jnp.swapaxes(cache_idx_k, 2, 3)
    mem_k_s = cache_mem_k.reshape(depth, bs, mem_tokens, mem_w)
    mem_v_s = cache_mem_v.reshape(depth, bs, mem_tokens, mem_w)
    h_s0 = state_h.reshape(depth, bs, 1, lru_w)
    lru_w_args = (conv_w, conv_b, w_rgate, b_rgate, w_igate, b_igate, lru_lambda)

    tm_p = _pick_tile(mp, 256)
    tm_mm = _pick_tile(mp, 1024)
    tm_out = _pick_tile(mp, 512)

    st_p, st_s, mk_list, mv_list = [], [], [], []
    for l in range(depth):
        m_kv = _norm_matmul(mem2d, mem_norm_g, w_mem_kv, l, _pick_tile(bp * mem_tokens, 512), 1024)
        mk = m_kv[:, :mem_w].reshape(bp, mem_tokens, mem_w)
        mv = m_kv[:, mem_w:].reshape(bp, mem_tokens, mem_w)
        mk_list.append(mk.reshape(bp, mem_tokens, MEM_HEADS, mem_hd))
        mv_list.append(mv.reshape(bp, mem_tokens, MEM_HEADS, mem_hd))

        xn, k, v, ki, kb, kib, q_t, qi_t, wi_t, vb_t = _proj_attn(xp, norm_g, w_in_t, tabs_p, l, tm_p, True)
        z_rest = _matmul(xn, w_in_t, l, rest0, n_rest, tm_mm, 1024, True)
        br_att = _prompt_attention(qi_t, wi_t, kib, q_t, kb, vb_t, z_rest, bp, seq)
        br_lru, h_last, conv_new = _lru(z_rest, conv0, h0, *lru_w_args, l, bp, seq, _pick_tile(seq, 256), BF16)
        br_mem = _mem_attention(z_rest, mk, mv, bp, seq, _pick_tile(seq, 512), BF16)
        merged = _merge([br_att, br_lru, br_mem], z_rest, w_branch, l, tm_out, 512)
        xp = _matmul(merged, w_out, l, 0, d, tm_out, 1024, False, residual=xp)
        st_p.append((k.reshape(bp, seq, KV_HEADS, HEAD_DIM), v.reshape(bp, seq, KV_HEADS, HEAD_DIM),
                     ki.reshape(bp, seq, IDX_DIM), conv_new, h_last.reshape(bp, lru_w)))

        xn, k, v, ki, kb, kib, q, qi, wi, vb = _proj_attn(xs, norm_g, w_in_t, tabs_s, l, ms, False)
        z_rest = _matmul(xn, w_in_t, l, rest0, n_rest, ms, 1024, True)
        qs = qi.reshape(ms * IDX_HEADS, IDX_DIM)
        ws = wi.reshape(ms * IDX_HEADS, 1)
        pad_rows = lambda a: jnp.pad(a.reshape(bs, t_new, a.shape[1]), ((0, 0), (0, LANES - t_new), (0, 0)))
        sc = _sample_index(page_table, qs, ws, pool_ik_t, l, bs, t_new, _pick_tile(n_pages, 2 * pages_per_step))
        bias, bias_new = _sample_select(sc, qs, ws, pad_rows(kib), bs, t_new, topk_s)
        br_att = _sample_attention(page_table, q, bias, bias_new, pad_rows(kb), pad_rows(vb), z_rest,
                                   pool_k, pool_v, l, bs, t_new, pages_per_step)
        br_lru, h_last, conv_new = _lru(z_rest, state_conv[l], h_s0[l], *lru_w_args, l, bs, t_new, t_new, F32)
        br_mem = _mem_attention(z_rest, mem_k_s[l], mem_v_s[l], bs, t_new, t_new, F32)
        merged = _merge([br_att, br_lru, br_mem], z_rest, w_branch, l, ms, 512)
        xs = _matmul(merged, w_out, l, 0, d, ms, 1024, False, residual=xs)
        st_s.append((k.reshape(bs, t_new, KV_HEADS, HEAD_DIM), v.reshape(bs, t_new, KV_HEADS, HEAD_DIM),
                     ki.reshape(bs, t_new, IDX_DIM), conv_new, h_last.reshape(bs, lru_w)))

    g_f = final_norm_g.reshape(1, d)
    y_prompt = _rmsnorm(xp, g_f, _pick_tile(mp, 512)).reshape(bp, seq, d)
    y_sample = _rmsnorm(xs, g_f, ms).reshape(bs, t_new, d)
    k_p, v_p, ik_p, conv_p, h_p = [jnp.stack(t) for t in zip(*st_p)]
    k_s, v_s, ik_s, conv_s, h_s = [jnp.stack(t) for t in zip(*st_s)]
    return (y_prompt, y_sample, k_p, v_p, ik_p, conv_p, h_p, jnp.stack(mk_list), jnp.stack(mv_list),
            k_s, v_s, ik_s, conv_s, h_s)
```

```python
import functools

import jax
import jax.numpy as jnp
from jax import lax
from jax.experimental import pallas as pl
from jax.experimental.pallas import tpu as pltpu

F32 = jnp.float32
BF16 = jnp.bfloat16
I32 = jnp.int32

N_HEADS = 8
HEAD_DIM = 128
KV_HEADS = 2
HEADS_PER_KV = N_HEADS // KV_HEADS
IDX_HEADS = 8
IDX_DIM = 64
INDEX_TOPK = 256
LRU_BLOCKS = 8
CONV_W = 4
LRU_C = 8.0
MEM_HEADS = 4
N_BRANCH = 3
ROPE_THETA = 10000.0
NORM_EPS = 1e-6

ATTN_W = N_HEADS * HEAD_DIM
KV_W = KV_HEADS * HEAD_DIM
QI_W = IDX_HEADS * IDX_DIM

LANES = 128
SUBLANES = 8
VMEM_LIMIT = 52 * 1024 * 1024
CAST_ROWS = 256
COUNT_ROWS = 8 * SUBLANES
LOG2E = 1.4426950408889634
QK_SCALE = HEAD_DIM ** -0.5 * LOG2E

NEG_BIG = -1e30
INT_MIN = -(2 ** 31)


def _cparams(n_axes):
    return pltpu.CompilerParams(dimension_semantics=("arbitrary",) * n_axes,
                                vmem_limit_bytes=VMEM_LIMIT)


def _sigmoid(x):
    return 1.0 / (1.0 + jnp.exp(-x))


def _silu(x):
    return x * _sigmoid(x)


def _dot(a, b):
    return jnp.dot(a, b, preferred_element_type=F32)


def _dot_nt(a, b):
    return lax.dot_general(a, b, (((1,), (1,)), ((), ())), preferred_element_type=F32)


def _rms(x, g):
    ms = jnp.mean(x * x, axis=-1, keepdims=True)
    return (x * lax.rsqrt(ms + NORM_EPS)) * g


def _cast_weight(dst_ref, src_ref):
    k = dst_ref.shape[0]
    for r in range(0, k, CAST_ROWS):
        rows = slice(r, min(r + CAST_ROWS, k))
        dst_ref[rows, :] = src_ref[rows, :].astype(BF16)


def _proj_attn_kernel(x_ref, g_ref, w_ref, cq_ref, sq_ref, ci_ref, si_ref,
                      xn_ref, k_ref, v_ref, ki_ref, kb_ref, kib_ref, q_ref, qi_ref, wi_ref, vb_ref,
                      wbf_ref, *, layer, transposed):
    @pl.when(pl.program_id(0) == 0)
    def _():
        _cast_weight(wbf_ref, w_ref)

    xnb = _rms(x_ref[...], g_ref[layer:layer + 1, :]).astype(BF16)
    xn_ref[...] = xnb
    z = _dot_nt(xnb, wbf_ref[...])

    cq, sq, ci, si = cq_ref[...], sq_ref[...], ci_ref[...], si_ref[...]
    lane = lax.broadcasted_iota(I32, cq.shape, 1)
    first_half = (lane & (IDX_DIM // 2)) == 0

    def rope128(t):
        return t * cq + pltpu.roll(t, HEAD_DIM // 2, 1) * sq

    def rope64(t):
        rot = jnp.where(first_half, pltpu.roll(t, LANES - IDX_DIM // 2, 1),
                        pltpu.roll(t, IDX_DIM // 2, 1))
        return t * ci + rot * si

    def tile(c):
        return z[:, c * LANES:(c + 1) * LANES]

    for h in range(N_HEADS):
        qh = rope128(tile(h)) * QK_SCALE
        if transposed:
            q_ref[h * HEAD_DIM:(h + 1) * HEAD_DIM, :] = qh.T.astype(BF16)
        else:
            q_ref[:, h * HEAD_DIM:(h + 1) * HEAD_DIM] = qh
    c0 = ATTN_W // LANES
    for h in range(KV_HEADS):
        sl = slice(h * HEAD_DIM, (h + 1) * HEAD_DIM)
        kr = rope128(tile(c0 + h))
        k_ref[:, sl] = kr
        kb_ref[:, sl] = kr.astype(BF16)
    c0 += KV_W // LANES
    for h in range(KV_HEADS):
        sl = slice(h * HEAD_DIM, (h + 1) * HEAD_DIM)
        vh = tile(c0 + h)
        v_ref[:, sl] = vh
        if transposed:
            vb_ref[sl, :] = vh.T.astype(BF16)
        else:
            vb_ref[:, sl] = vh.astype(BF16)
    c0 += KV_W // LANES
    for c in range(QI_W // LANES):
        qc = rope64(tile(c0 + c))
        if transposed:
            qi_ref[c * LANES:(c + 1) * LANES, :] = qc.T.astype(BF16)
        else:
            qi_ref[:, c * LANES:(c + 1) * LANES] = qc.astype(BF16)
    c0 += QI_W // LANES
    last = tile(c0)
    kir = rope64(last)[:, :IDX_DIM]
    ki_ref[...] = kir
    kib_ref[...] = kir.astype(BF16)
    if transposed:
        wi_ref[...] = last.T[IDX_DIM:IDX_DIM + IDX_HEADS, :] * (IDX_HEADS ** -0.5)
    else:
        wi_ref[...] = last[:, IDX_DIM:IDX_DIM + IDX_HEADS] * (IDX_HEADS ** -0.5)


def _proj_attn(x, norm_g, w_in_t, tabs, layer, tm, transposed):
    m, d = x.shape
    na = ATTN_W + 2 * KV_W + QI_W + LANES
    row = lambda i: (i, 0)
    col = lambda i: (0, i)
    const = lambda i: (0, 0)

    def tok(width, dtype):
        return ((m, width), dtype, pl.BlockSpec((tm, width), row))

    def feat(width, dtype):
        return ((width, m), dtype, pl.BlockSpec((width, tm), col))

    outs = [tok(d, BF16), tok(KV_W, F32), tok(KV_W, F32), tok(IDX_DIM, F32), tok(KV_W, BF16), tok(IDX_DIM, BF16)]
    if transposed:
        outs += [feat(ATTN_W, BF16), feat(QI_W, BF16), feat(IDX_HEADS, F32), feat(KV_W, BF16)]
    else:
        outs += [tok(ATTN_W, F32), tok(QI_W, BF16), tok(IDX_HEADS, F32), tok(KV_W, BF16)]
    return pl.pallas_call(
        functools.partial(_proj_attn_kernel, layer=layer, transposed=transposed),
        grid=(m // tm,),
        in_specs=[pl.BlockSpec((tm, d), row), pl.BlockSpec(norm_g.shape, const),
                  pl.BlockSpec((None, na, d), lambda i: (layer, 0, 0), pipeline_mode=pl.Buffered(1))]
                 + [pl.BlockSpec((tm, LANES), row)] * 4,
        out_specs=[o[2] for o in outs],
        out_shape=[jax.ShapeDtypeStruct(o[0], o[1]) for o in outs],
        scratch_shapes=[pltpu.VMEM((na, d), BF16)],
        compiler_params=_cparams(1),
        name="proj_attn",
    )(x, norm_g, w_in_t, *tabs)


def _mm_kernel(*refs, w_transposed, residual):
    x_ref, w_ref = refs[0], refs[1]
    r_ref = refs[2] if residual else None
    o_ref, wbf_ref = refs[-2], refs[-1]

    @pl.when(pl.program_id(1) == 0)
    def _():
        _cast_weight(wbf_ref, w_ref.at[0] if w_transposed else w_ref)

    out = _dot_nt(x_ref[...], wbf_ref[...]) if w_transposed else _dot(x_ref[...], wbf_ref[...])
    if residual:
        out = r_ref[...] + out
    o_ref[...] = out


def _matmul(x, w, layer, col0, n, tm, tn, w_transposed, residual=None):
    m, k = x.shape
    assert n % tn == 0 and m % tm == 0
    if w_transposed:
        assert col0 % SUBLANES == 0
        w_spec = pl.BlockSpec((pl.Element(1), pl.Element(tn), pl.Element(k)),
                              lambda j, i: (layer, (col0 // SUBLANES + j * (tn // SUBLANES)) * SUBLANES, 0))
        w_scratch = pltpu.VMEM((tn, k), BF16)
    else:
        assert col0 % tn == 0
        w_spec = pl.BlockSpec((None, k, tn), lambda j, i: (layer, 0, col0 // tn + j))
        w_scratch = pltpu.VMEM((k, tn), BF16)
    in_specs = [pl.BlockSpec((tm, k), lambda j, i: (i, 0)), w_spec]
    args = [x, w]
    if residual is not None:
        in_specs.append(pl.BlockSpec((tm, tn), lambda j, i: (i, j)))
        args.append(residual)
    return pl.pallas_call(
        functools.partial(_mm_kernel, w_transposed=w_transposed, residual=residual is not None),
        grid=(n // tn, m // tm),
        in_specs=in_specs,
        out_specs=pl.BlockSpec((tm, tn), lambda j, i: (i, j)),
        out_shape=jax.ShapeDtypeStruct((m, n), F32),
        scratch_shapes=[w_scratch],
        compiler_params=_cparams(2),
        name="matmul_residual" if residual is not None else "matmul",
    )(*args)


def _norm_mm_kernel(x_ref, g_ref, w_ref, o_ref, *, layer):
    xn = _rms(x_ref[...], g_ref[layer:layer + 1, :]).astype(BF16)
    o_ref[...] = _dot(xn, w_ref[...].astype(BF16))


def _norm_matmul(x, g, w, layer, tm, tn):
    m, k = x.shape
    n = w.shape[2]
    return pl.pallas_call(
        functools.partial(_norm_mm_kernel, layer=layer),
        grid=(n // tn, m // tm),
        in_specs=[pl.BlockSpec((tm, k), lambda j, i: (i, 0)), pl.BlockSpec(g.shape, lambda j, i: (0, 0)),
                  pl.BlockSpec((None, k, tn), lambda j, i: (layer, 0, j))],
        out_specs=pl.BlockSpec((tm, tn), lambda j, i: (i, j)),
        out_shape=jax.ShapeDtypeStruct((m, n), F32),
        compiler_params=_cparams(2),
        name="norm_matmul",
    )(x, g, w)


def _rmsnorm_kernel(x_ref, g_ref, o_ref):
    o_ref[...] = _rms(x_ref[...], g_ref[...])


def _rmsnorm(x, g, tm):
    m, d = x.shape
    return pl.pallas_call(
        _rmsnorm_kernel,
        grid=(m // tm,),
        in_specs=[pl.BlockSpec((tm, d), lambda i: (i, 0)), pl.BlockSpec((1, d), lambda i: (0, 0))],
        out_specs=pl.BlockSpec((tm, d), lambda i: (i, 0)),
        out_shape=jax.ShapeDtypeStruct((m, d), F32),
        compiler_params=_cparams(1),
        name="final_rmsnorm",
    )(x, g)


def _sortable_key(score):
    bits = pltpu.bitcast(score, I32)
    return bits ^ ((bits >> 31) & jnp.int32(0x7FFFFFFF))


def _kth_largest_key(count_ge, shape, topk):
    zero = jnp.zeros(shape, I32)
    ans = jnp.where(count_ge(zero) >= topk, zero, jnp.full(shape, INT_MIN, I32))

    def bit_body(j, ans):
        cand = ans | lax.shift_left(jnp.int32(1), 30 - j)
        return jnp.where(count_ge(cand) >= topk, cand, ans)

    return lax.fori_loop(0, 31, bit_body, ans)


def _tie_cutoff(count_tie_lt, need, shape, idx_bits):
    def bit_body(j, cut):
        cand = cut | lax.shift_left(jnp.int32(1), idx_bits - 1 - j)
        return jnp.where(count_tie_lt(cand) < need, cand, cut)

    return lax.fori_loop(0, idx_bits, bit_body, jnp.zeros(shape, I32))


def _prompt_attn_kernel(qi_ref, wi_ref, kib_ref, q_ref, kb_ref, vb_ref, ga_ref, o_ref,
                        key_ref, bias_ref, acc_ref, *, tq, kc, topk, seq):
    i = pl.program_id(1)
    nch = ((i + 1) * tq + kc - 1) // kc
    key_pos = lax.broadcasted_iota(I32, (kc, tq), 0)
    q_pos = i * tq + lax.broadcasted_iota(I32, (kc, tq), 1)
    idx_bits = max(1, (seq - 1).bit_length())

    def chunk(ref, c):
        return ref[pl.ds(pl.multiple_of(c * kc, kc), kc), :]

    wi = wi_ref[...] * (IDX_DIM ** -0.5)

    def score_body(c, carry):
        kic = chunk(kib_ref, c)
        acc = jnp.zeros((kc, tq), F32)
        for h in range(IDX_HEADS):
            s = _dot(kic, qi_ref[h * IDX_DIM:(h + 1) * IDX_DIM, :])
            acc = acc + jnp.maximum(s, 0.0) * wi[h:h + 1, :]
        key = jnp.where(c * kc + key_pos <= q_pos, _sortable_key(acc), INT_MIN)
        key_ref[pl.ds(pl.multiple_of(c * kc, kc), kc), :] = key
        return carry

    lax.fori_loop(0, nch, score_body, 0)

    need_search = (i + 1) * tq > topk
    pos_tile = key_pos[:, :LANES]

    def select_keys(sl):
        def chunk_count(pred):
            def body(c, acc):
                k = key_ref[pl.ds(pl.multiple_of(c * kc, kc), kc), sl]
                ind = jnp.where(pred(k, c * kc + pos_tile), 1.0, 0.0)
                return acc + jnp.sum(ind.reshape(kc // COUNT_ROWS, COUNT_ROWS, LANES), axis=0)
            acc = lax.fori_loop(0, nch, body, jnp.zeros((COUNT_ROWS, LANES), F32))
            return jnp.sum(acc, axis=0, keepdims=True)

        def count_ge(cand):
            return chunk_count(lambda k, pos: k >= cand)

        def write_bias(keep):
            def body(c, carry):
                rows = pl.ds(pl.multiple_of(c * kc, kc), kc)
                bias_ref[rows, sl] = jnp.where(keep(key_ref[rows, sl], c * kc + pos_tile), 0.0, NEG_BIG)
                return carry
            lax.fori_loop(0, nch, body, 0)

        tile1 = (1, LANES)
        thr = lax.cond(need_search,
                       lambda: jnp.maximum(_kth_largest_key(count_ge, tile1, topk), INT_MIN + 1),
                       lambda: jnp.full(tile1, INT_MIN + 1, I32))
        write_bias(lambda k, pos: k >= thr)

        @pl.when(jnp.max(count_ge(thr)) > topk)
        def _():
            need = topk - count_ge(thr + 1)
            count_tie_lt = lambda cut: chunk_count(lambda k, pos: jnp.where(k == thr, pos, seq) < cut)
            cut = _tie_cutoff(count_tie_lt, need, tile1, idx_bits)
            write_bias(lambda k, pos: (k > thr) | ((k == thr) & (pos <= cut)))

    for lt in range(tq // LANES):
        select_keys(slice(lt * LANES, (lt + 1) * LANES))

    qw = HEADS_PER_KV * tq
    q4 = [jnp.concatenate([q_ref[(g * HEADS_PER_KV + hh) * HEAD_DIM:(g * HEADS_PER_KV + hh + 1) * HEAD_DIM, :]
                           for hh in range(HEADS_PER_KV)], axis=1) for g in range(KV_HEADS)]
    acc_ref[...] = jnp.zeros(acc_ref.shape, F32)

    def attn_body(c, carry):
        off = pl.multiple_of(c * kc, kc)
        bias = chunk(bias_ref, c)
        bias4 = jnp.concatenate([bias] * HEADS_PER_KV, axis=1)
        out = []
        for g in range(KV_HEADS):
            m_prev, l_prev = carry[g]
            hs = slice(g * HEAD_DIM, (g + 1) * HEAD_DIM)
            s = _dot(kb_ref[pl.ds(off, kc), hs], q4[g]) + bias4
            m_next = jnp.maximum(m_prev, jnp.max(s, axis=0, keepdims=True))
            p = jnp.exp2(s - m_next)
            alpha = jnp.exp2(m_prev - m_next)
            l_next = alpha * l_prev + jnp.sum(p, axis=0, keepdims=True)
            acc_ref[g] = acc_ref[g] * alpha + _dot(vb_ref[hs, pl.ds(off, kc)], p.astype(BF16))
            out.append((m_next, l_next))
        return tuple(out)

    init = tuple((jnp.full((1, qw), NEG_BIG, F32), jnp.zeros((1, qw), F32)) for _ in range(KV_HEADS))
    stats = lax.fori_loop(0, nch, attn_body, init)
    for g in range(KV_HEADS):
        o_t = acc_ref[g] / stats[g][1]
        for hh in range(HEADS_PER_KV):
            sl = slice((g * HEADS_PER_KV + hh) * HEAD_DIM, (g * HEADS_PER_KV + hh + 1) * HEAD_DIM)
            o_ref[:, sl] = (o_t[:, hh * tq:(hh + 1) * tq].T * _silu(ga_ref[:, sl])).astype(BF16)


def _prompt_attention(qi_t, wi_t, kib, q_t, kb, vb_t, z_rest, batch, seq):
    tq = _pick_tile(seq, 2 * LANES)
    nq = seq // tq
    kc = _pick_tile(seq, 512)
    topk = min(INDEX_TOPK, seq // 4)
    blk_t = lambda b, i: (0, b * nq + i)
    blk = lambda b, i: (b * nq + i, 0)
    return pl.pallas_call(
        functools.partial(_prompt_attn_kernel, tq=tq, kc=kc, topk=topk, seq=seq),
        grid=(batch, nq),
        in_specs=[pl.BlockSpec((QI_W, tq), blk_t), pl.BlockSpec((IDX_HEADS, tq), blk_t),
                  pl.BlockSpec((seq, IDX_DIM), lambda b, i: (b, 0)), pl.BlockSpec((ATTN_W, tq), blk_t),
                  pl.BlockSpec((seq, KV_W), lambda b, i: (b, 0)), pl.BlockSpec((KV_W, seq), lambda b, i: (0, b)),
                  pl.BlockSpec((tq, ATTN_W), blk)],
        out_specs=pl.BlockSpec((tq, ATTN_W), blk),
        out_shape=jax.ShapeDtypeStruct((batch * seq, ATTN_W), BF16),
        scratch_shapes=[pltpu.VMEM((seq, tq), I32), pltpu.VMEM((seq, tq), F32),
                        pltpu.VMEM((KV_HEADS, HEAD_DIM, HEADS_PER_KV * tq), F32)],
        compiler_params=_cparams(2),
        name="prompt_attention",
    )(qi_t, wi_t, kib, q_t, kb, vb_t, z_rest)


def _sample_index_kernel(pt_ref, qs_ref, ws_ref, *rest, pages_per_step, t_new):
    page_refs, o_ref = rest[:pages_per_step], rest[pages_per_step]
    qs = qs_ref[...]
    ws = ws_ref[...] * (IDX_DIM ** -0.5)
    page = page_refs[0].shape[1]
    wsb = jnp.broadcast_to(ws, (t_new * IDX_HEADS, page))
    for j in range(pages_per_step):
        kp = page_refs[j][...].astype(BF16)
        s = jnp.maximum(_dot(qs, kp), 0.0) * wsb
        o_ref[:, j * page:(j + 1) * page] = jnp.sum(s.reshape(t_new, IDX_HEADS, page), axis=1)


def _sample_index(page_table, qs, ws, pool_ik_t, layer, dec_batch, t_new, pages_per_step):
    n_pages = page_table.shape[1]
    page = pool_ik_t.shape[3]
    steps = n_pages // pages_per_step
    rows = t_new * IDX_HEADS

    def page_map(j):
        return lambda b, s, pt: (layer, pt[b, s * pages_per_step + j], 0, 0)

    grid_spec = pltpu.PrefetchScalarGridSpec(
        num_scalar_prefetch=1,
        grid=(dec_batch, steps),
        in_specs=[pl.BlockSpec((rows, IDX_DIM), lambda b, s, pt: (b, 0)),
                  pl.BlockSpec((rows, 1), lambda b, s, pt: (b, 0))]
                 + [pl.BlockSpec((None, None, IDX_DIM, page), page_map(j)) for j in range(pages_per_step)],
        out_specs=pl.BlockSpec((t_new, pages_per_step * page), lambda b, s, pt: (b, s)),
    )
    return pl.pallas_call(
        functools.partial(_sample_index_kernel, pages_per_step=pages_per_step, t_new=t_new),
        grid_spec=grid_spec,
        out_shape=jax.ShapeDtypeStruct((dec_batch * t_new, n_pages * page), F32),
        compiler_params=_cparams(2),
        name="sample_index",
    )(page_table, qs, ws, *([pool_ik_t] * pages_per_step))


def _sample_select_kernel(sc_ref, qs_ref, ws_ref, kn_ref, bias_ref, bias_new_ref, key_ref, keyn_ref,
                          *, dec_batch, t_new, past, topk):
    rows = dec_batch * t_new
    hrows = t_new * IDX_HEADS
    ws = ws_ref[...] * (IDX_DIM ** -0.5)
    lane_n = lax.broadcasted_iota(I32, (rows, LANES), 1)
    row_n = lax.broadcasted_iota(I32, (rows, LANES), 0) % t_new
    s_new = []
    for b in range(dec_batch):
        hs = slice(b * hrows, (b + 1) * hrows)
        s = jnp.maximum(_dot_nt(qs_ref[hs, :], kn_ref[b]), 0.0) * jnp.broadcast_to(ws[hs, :], (hrows, LANES))
        s_new.append(jnp.sum(s.reshape(t_new, IDX_HEADS, LANES), axis=1))
    s_new = jnp.concatenate(s_new, axis=0) + 0.0
    keyn_ref[...] = jnp.where(lane_n <= row_n, _sortable_key(s_new), INT_MIN)
    key_ref[...] = _sortable_key(sc_ref[...] + 0.0)
    pos = lax.broadcasted_iota(I32, (rows, past), 1)
    idx_bits = (past + LANES - 1).bit_length()
    col1 = (rows, 1)

    def count(pred):
        a = jnp.sum(jnp.where(pred(key_ref[...], pos), 1.0, 0.0), axis=1, keepdims=True)
        b = jnp.sum(jnp.where(pred(keyn_ref[...], past + lane_n), 1.0, 0.0), axis=1, keepdims=True)
        return a + b

    count_ge = lambda cand: count(lambda k, p: k >= cand)
    thr = jnp.maximum(_kth_largest_key(count_ge, col1, topk), INT_MIN + 1)
    bias_ref[...] = jnp.where(key_ref[...] >= thr, 0.0, NEG_BIG)
    bias_new_ref[...] = jnp.where(keyn_ref[...] >= thr, 0.0, NEG_BIG)

    @pl.when(jnp.max(count_ge(thr)) > topk)
    def _():
        need = topk - count_ge(thr + 1)
        big = jnp.int32(2 ** idx_bits)
        count_tie_lt = lambda cut: count(lambda k, p: jnp.where(k == thr, p, big) < cut)
        cut = _tie_cutoff(count_tie_lt, need, col1, idx_bits)
        keep = lambda k, p: (k > thr) | ((k == thr) & (p <= cut))
        bias_ref[...] = jnp.where(keep(key_ref[...], pos), 0.0, NEG_BIG)
        bias_new_ref[...] = jnp.where(keep(keyn_ref[...], past + lane_n), 0.0, NEG_BIG)


def _sample_select(sc, qs, ws, knew_pad, dec_batch, t_new, topk):
    rows, past = sc.shape
    whole = lambda a: pl.BlockSpec(a.shape, lambda i: (0,) * a.ndim)
    out_shape = [jax.ShapeDtypeStruct((rows, past), F32), jax.ShapeDtypeStruct((rows, LANES), F32)]
    return pl.pallas_call(
        functools.partial(_sample_select_kernel, dec_batch=dec_batch, t_new=t_new, past=past, topk=topk),
        grid=(1,),
        in_specs=[whole(sc), whole(qs), whole(ws), whole(knew_pad)],
        out_specs=[whole(o) for o in out_shape],
        out_shape=out_shape,
        scratch_shapes=[pltpu.VMEM((rows, past), I32), pltpu.VMEM((rows, LANES), I32)],
        compiler_params=_cparams(1),
        name="sample_select",
    )(sc, qs, ws, knew_pad)


def _sample_attn_kernel(pt_ref, q_ref, bias_ref, biasn_ref, kn_ref, vn_ref, ga_ref, *rest,
                        pages_per_step, t_new):
    k_refs = rest[:pages_per_step]
    v_refs = rest[pages_per_step:2 * pages_per_step]
    o_ref, m_ref, l_ref, acc_ref = rest[2 * pages_per_step:]
    step = pl.program_id(1)
    page = k_refs[0].shape[0] // KV_HEADS

    @pl.when(step == 0)
    def _():
        m_ref[...] = jnp.full(m_ref.shape, NEG_BIG, F32)
        l_ref[...] = jnp.zeros(l_ref.shape, F32)
        acc_ref[...] = jnp.zeros(acc_ref.shape, F32)

    def q_group(g):
        return jnp.concatenate(
            [q_ref[:, (g * HEADS_PER_KV + hh) * HEAD_DIM:(g * HEADS_PER_KV + hh + 1) * HEAD_DIM]
             for hh in range(HEADS_PER_KV)], axis=0).astype(BF16)

    def head_rows(ref, g):
        return ref[pl.ds(g, page, stride=KV_HEADS), :].astype(BF16)

    def update(g, s, v_of):
        m_prev = m_ref[g]
        m_next = jnp.maximum(m_prev, jnp.max(s, axis=1, keepdims=True))
        n = s.shape[1] // LANES
        p = jnp.exp2(s - jnp.concatenate([m_next] * n, axis=1))
        alpha = jnp.exp2(m_prev - m_next)
        l_ref[g] = alpha * l_ref[g] + jnp.sum(p, axis=1, keepdims=True)
        pv = _dot(p[:, :page].astype(BF16), v_of(0))
        for j in range(1, s.shape[1] // page):
            pv = pv + _dot(p[:, j * page:(j + 1) * page].astype(BF16), v_of(j))
        acc_ref[g] = acc_ref[g] * alpha + pv
        m_ref[g] = m_next

    bias4 = jnp.concatenate([bias_ref[...]] * HEADS_PER_KV, axis=0)
    for g in range(KV_HEADS):
        qg = q_group(g)
        s = jnp.concatenate([_dot_nt(qg, head_rows(k_refs[j], g)) for j in range(pages_per_step)], axis=1)
        update(g, s + bias4, lambda j: head_rows(v_refs[j], g))

    @pl.when(step == pl.num_programs(1) - 1)
    def _():
        biasn4 = jnp.concatenate([biasn_ref[...]] * HEADS_PER_KV, axis=0)
        for g in range(KV_HEADS):
            hs = slice(g * HEAD_DIM, (g + 1) * HEAD_DIM)
            s = _dot_nt(q_group(g), kn_ref[:, hs]) + biasn4
            update(g, s, lambda j: vn_ref[:, hs])
            o = acc_ref[g] / l_ref[g]
            for hh in range(HEADS_PER_KV):
                sl = slice((g * HEADS_PER_KV + hh) * HEAD_DIM, (g * HEADS_PER_KV + hh + 1) * HEAD_DIM)
                o_ref[:, sl] = o[hh * t_new:(hh + 1) * t_new] * _silu(ga_ref[:, sl])


def _sample_attention(page_table, q, bias, bias_new, knew_pad, vnew_pad, z_rest, pool_k, pool_v, layer,
                      dec_batch, t_new, pages_per_step):
    n_pages = page_table.shape[1]
    page_rows = pool_k.shape[2]
    page = page_rows // KV_HEADS
    steps = n_pages // pages_per_step
    rows = HEADS_PER_KV * t_new
    per_b = lambda b, s, pt: (b, 0)

    def page_map(j):
        return lambda b, s, pt: (layer, pt[b, s * pages_per_step + j], 0, 0)

    page_specs = [pl.BlockSpec((None, None, page_rows, HEAD_DIM), page_map(j)) for j in range(pages_per_step)]
    grid_spec = pltpu.PrefetchScalarGridSpec(
        num_scalar_prefetch=1,
        grid=(dec_batch, steps),
        in_specs=[pl.BlockSpec((t_new, ATTN_W), per_b),
                  pl.BlockSpec((t_new, pages_per_step * page), lambda b, s, pt: (b, s)),
                  pl.BlockSpec((t_new, LANES), per_b),
                  pl.BlockSpec((None, LANES, KV_W), lambda b, s, pt: (b, 0, 0)),
                  pl.BlockSpec((None, LANES, KV_W), lambda b, s, pt: (b, 0, 0)),
                  pl.BlockSpec((t_new, ATTN_W), per_b)] + page_specs + page_specs,
        out_specs=pl.BlockSpec((t_new, ATTN_W), per_b),
        scratch_shapes=[pltpu.VMEM((KV_HEADS, rows, LANES), F32), pltpu.VMEM((KV_HEADS, rows, LANES), F32),
                        pltpu.VMEM((KV_HEADS, rows, HEAD_DIM), F32)],
    )
    return pl.pallas_call(
        functools.partial(_sample_attn_kernel, pages_per_step=pages_per_step, t_new=t_new),
        grid_spec=grid_spec,
        out_shape=jax.ShapeDtypeStruct((dec_batch * t_new, ATTN_W), F32),
        compiler_params=_cparams(2),
        name="sample_attention",
    )(page_table, q, bias, bias_new, knew_pad, vnew_pad, z_rest,
      *([pool_k] * pages_per_step), *([pool_v] * pages_per_step))


def _lru_kernel(xr_ref, gr_ref, cp_ref, hp_ref, cw_ref, cb_ref, wr_ref, br_ref, wg_ref, bg_ref, lam_ref,
                o_ref, hl_ref, cn_ref, ext_ref, hc_ref, a_ref, b_ref, *, tt, layer):
    t = pl.program_id(1)
    tail = CONV_W - 1
    base = SUBLANES
    width = xr_ref.shape[1]
    bw = width // LRU_BLOCKS
    vec = lambda ref: ref[layer:layer + 1, :]

    @pl.when(t == 0)
    def _():
        ext_ref[base - tail:base, :] = cp_ref[...]
        hc_ref[...] = hp_ref[...]

    x = xr_ref[...]
    ext_ref[base:base + tt, :] = x
    cw = cw_ref[...]
    xc = vec(cb_ref) + x * cw[tail:tail + 1, :]
    for j in range(tail):
        xc = xc + ext_ref[base - tail + j: base - tail + j + tt, :] * cw[j:j + 1, :]
    new_tail = x[tt - tail:, :]
    ext_ref[base - tail:base, :] = new_tail

    xcb = xc.astype(BF16)
    r_lin = jnp.concatenate([_dot(xcb[:, n * bw:(n + 1) * bw], wr_ref[n].astype(BF16))
                             for n in range(LRU_BLOCKS)], axis=1)
    g_lin = jnp.concatenate([_dot(xcb[:, n * bw:(n + 1) * bw], wg_ref[n].astype(BF16))
                             for n in range(LRU_BLOCKS)], axis=1)
    r = _sigmoid(r_lin + vec(br_ref))
    gi = _sigmoid(g_lin + vec(bg_ref))
    nl = -vec(lam_ref)
    softplus = jnp.maximum(nl, 0.0) + jnp.log1p(jnp.exp(-jnp.abs(nl)))
    log_a = (-LRU_C) * r * softplus
    a = jnp.exp(log_a)
    b = jnp.sqrt(jnp.tanh(-log_a) * (a * a + 1.0)) * (gi * xc)

    row = lax.broadcasted_iota(I32, (tt, width), 0) & (SUBLANES - 1)
    d = 1
    while d < SUBLANES:
        keep = row >= d
        a_s = jnp.where(keep, pltpu.roll(a, d, 0), 1.0)
        b_s = jnp.where(keep, pltpu.roll(b, d, 0), 0.0)
        b = a * b_s + b
        a = a * a_s
        d *= 2
    a_ref[...] = a
    b_ref[...] = b

    def group_body(gidx, h):
        off = pl.multiple_of(gidx * SUBLANES, SUBLANES)
        h8 = a_ref[pl.ds(off, SUBLANES), :] * h + b_ref[pl.ds(off, SUBLANES), :]
        b_ref[pl.ds(off, SUBLANES), :] = h8
        return h8[SUBLANES - 1:SUBLANES, :]

    h_last = lax.fori_loop(0, tt // SUBLANES, group_body, hc_ref[...])
    hc_ref[...] = h_last
    o_ref[...] = (b_ref[...] * _silu(gr_ref[...])).astype(o_ref.dtype)

    @pl.when(t == pl.num_programs(1) - 1)
    def _():
        hl_ref[...] = h_last
        cn_ref[...] = new_tail


def _lru(z_rest, conv_prev, h_prev, conv_w, conv_b, w_r, b_r, w_g, b_g, lam, layer, batch, t_len, tt, out_dtype):
    width = conv_w.shape[2]
    nt = t_len // tt
    tail = CONV_W - 1
    bw = width // LRU_BLOCKS
    whole = lambda a: pl.BlockSpec(a.shape, lambda b, t: (0,) * a.ndim)
    per_b = lambda b, t: (b, 0, 0)
    return pl.pallas_call(
        functools.partial(_lru_kernel, tt=tt, layer=layer),
        grid=(batch, nt),
        in_specs=[pl.BlockSpec((tt, width), lambda b, t: (b * nt + t, 1)),
                  pl.BlockSpec((tt, width), lambda b, t: (b * nt + t, 2)),
                  pl.BlockSpec((None, tail, width), per_b), pl.BlockSpec((None, 1, width), per_b),
                  pl.BlockSpec((None, CONV_W, width), lambda b, t: (layer, 0, 0)), whole(conv_b),
                  pl.BlockSpec((None, LRU_BLOCKS, bw, bw), lambda b, t: (layer, 0, 0, 0)), whole(b_r),
                  pl.BlockSpec((None, LRU_BLOCKS, bw, bw), lambda b, t: (layer, 0, 0, 0)), whole(b_g),
                  whole(lam)],
        out_specs=[pl.BlockSpec((tt, width), lambda b, t: (b * nt + t, 0)),
                   pl.BlockSpec((None, 1, width), per_b), pl.BlockSpec((None, tail, width), per_b)],
        out_shape=[jax.ShapeDtypeStruct((batch * t_len, width), out_dtype),
                   jax.ShapeDtypeStruct((batch, 1, width), F32),
                   jax.ShapeDtypeStruct((batch, tail, width), F32)],
        scratch_shapes=[pltpu.VMEM((SUBLANES + tt, width), F32), pltpu.VMEM((1, width), F32),
                        pltpu.VMEM((tt, width), F32), pltpu.VMEM((tt, width), F32)],
        compiler_params=_cparams(2),
        name="rglru",
    )(z_rest, z_rest, conv_prev, h_prev, conv_w, conv_b, w_r, b_r, w_g, b_g, lam)


def _mem_attn_kernel(qm_ref, gm_ref, mk_ref, mv_ref, o_ref):
    width = qm_ref.shape[1]
    hd = width // MEM_HEADS
    scale = hd ** -0.5
    for h in range(MEM_HEADS):
        sl = slice(h * hd, (h + 1) * hd)
        s = _dot_nt(qm_ref[:, sl].astype(BF16), mk_ref[:, sl].astype(BF16)) * scale
        p = jnp.exp(s - jnp.max(s, axis=1, keepdims=True))
        l = jnp.sum(p, axis=1, keepdims=True)
        o = _dot(p.astype(BF16), mv_ref[:, sl].astype(BF16)) / l
        o_ref[:, sl] = (o * _silu(gm_ref[:, sl])).astype(o_ref.dtype)


def _mem_attention(z_rest, mk, mv, batch, t_len, tm, out_dtype):
    width = mk.shape[2]
    mem = mk.shape[1]
    nt = t_len // tm
    return pl.pallas_call(
        _mem_attn_kernel,
        grid=(batch, nt),
        in_specs=[pl.BlockSpec((tm, width), lambda b, t: (b * nt + t, 3)),
                  pl.BlockSpec((tm, width), lambda b, t: (b * nt + t, 4)),
                  pl.BlockSpec((None, mem, width), lambda b, t: (b, 0, 0)),
                  pl.BlockSpec((None, mem, width), lambda b, t: (b, 0, 0))],
        out_specs=pl.BlockSpec((tm, width), lambda b, t: (b * nt + t, 0)),
        out_shape=jax.ShapeDtypeStruct((batch * t_len, width), out_dtype),
        compiler_params=_cparams(2),
        name="mem_attention",
    )(z_rest, z_rest, mk, mv)


def _merge_kernel(ba_ref, bl_ref, bm_ref, ga_ref, gl_ref, gm_ref, wa_ref, wl_ref, wm_ref, o_ref, wbf_ref):
    @pl.when(pl.program_id(1) == 0)
    def _():
        for b, w_ref in enumerate((wa_ref, wl_ref, wm_ref)):
            _cast_weight(wbf_ref.at[b], w_ref)

    out = _sigmoid(ga_ref[...]) * _dot(ba_ref[...].astype(BF16), wbf_ref[0])
    out = out + _sigmoid(gl_ref[...]) * _dot(bl_ref[...].astype(BF16), wbf_ref[1])
    out = out + _sigmoid(gm_ref[...]) * _dot(bm_ref[...].astype(BF16), wbf_ref[2])
    o_ref[...] = out.astype(BF16)


def _merge(branches, z_rest, w_branch, layer, tm, tn):
    m, bw = branches[0].shape
    d = w_branch.shape[3]
    gate_col0 = 5 * bw // tn
    br_spec = pl.BlockSpec((tm, bw), lambda j, i: (i, 0))

    def gate_spec(b):
        return pl.BlockSpec((tm, tn), lambda j, i: (i, gate_col0 + b * (d // tn) + j))

    def w_spec(b):
        return pl.BlockSpec((None, None, bw, tn), lambda j, i: (layer, b, 0, j))

    return pl.pallas_call(
        _merge_kernel,
        grid=(d // tn, m // tm),
        in_specs=[br_spec] * N_BRANCH + [gate_spec(b) for b in range(N_BRANCH)]
                 + [w_spec(b) for b in range(N_BRANCH)],
        out_specs=pl.BlockSpec((tm, tn), lambda j, i: (i, j)),
        out_shape=jax.ShapeDtypeStruct((m, d), BF16),
        scratch_shapes=[pltpu.VMEM((N_BRANCH, bw, tn), BF16)],
        compiler_params=_cparams(2),
        name="merge",
    )(*branches, z_rest, z_rest, z_rest, w_branch, w_branch, w_branch)


def _rope_tables(pos, reps):
    def tab(dim):
        half = dim // 2
        freq = ROPE_THETA ** (-jnp.arange(half, dtype=F32) / half)
        ang = pos.astype(F32)[:, None] * freq[None, :]
        cos, sin = jnp.cos(ang), jnp.sin(ang)
        n = LANES // dim
        c = jnp.tile(jnp.concatenate([cos, cos], axis=1), (reps, n))
        s = jnp.tile(jnp.concatenate([-sin, sin], axis=1), (reps, n))
        return c, s
    cq, sq = tab(HEAD_DIM)
    ci, si = tab(IDX_DIM)
    return cq, sq, ci, si


def _pick_tile(m, pref):
    t = min(m, pref)
    while m % t:
        t //= 2
    return t


def kernel(x_prompt, x_sample, mem_prompt, cache_k, cache_v, cache_idx_k, cache_mem_k, cache_mem_v, state_conv, state_h, page_table, norm_g, w_in, conv_w, conv_b, w_rgate, b_rgate, w_igate, b_igate, lru_lambda, mem_norm_g, w_mem_kv, w_branch, w_out, final_norm_g):
    bp, seq, d = x_prompt.shape
    bs, t_new, _ = x_sample.shape
    depth = w_in.shape[0]
    n_pool, page = cache_k.shape[1], cache_k.shape[2]
    n_pages = page_table.shape[1]
    past = n_pages * page
    mem_tokens = mem_prompt.shape[1]
    lru_w = conv_w.shape[2]
    mem_w = w_mem_kv.shape[2] // 2
    mem_hd = mem_w // MEM_HEADS
    rest0 = ATTN_W + 2 * KV_W + QI_W + IDX_DIM + IDX_HEADS
    n_rest = w_in.shape[2] - rest0
    topk_s = min(INDEX_TOPK, (past + t_new) // 4)
    pages_per_step = _pick_tile(n_pages, 16)

    tabs_p = _rope_tables(jnp.arange(seq), bp)
    tabs_s = _rope_tables(past + jnp.arange(t_new), bs)

    mp, ms = bp * seq, bs * t_new
    xp = x_prompt.reshape(mp, d)
    xs = x_sample.reshape(ms, d)
    mem2d = mem_prompt.reshape(bp * mem_tokens, d)
    conv0 = jnp.zeros((bp, CONV_W - 1, lru_w), F32)
    h0 = jnp.zeros((bp, 1, lru_w), F32)
    pool_k = cache_k.reshape(depth, n_pool, page * KV_HEADS, HEAD_DIM)
    pool_v = cache_v.reshape(depth, n_pool, page * KV_HEADS, HEAD_DIM)
    w_in_t = jnp.swapaxes(w_in, 1, 2)
    pool_ik_t = jnp.swapaxes(cache_idx_k, 2, 3)
    mem_k_s = cache_mem_k.reshape(depth, bs, mem_tokens, mem_w)
    mem_v_s = cache_mem_v.reshape(depth, bs, mem_tokens, mem_w)
    h_s0 = state_h.reshape(depth, bs, 1, lru_w)
    lru_w_args = (conv_w, conv_b, w_rgate, b_rgate, w_igate, b_igate, lru_lambda)

    tm_p = _pick_tile(mp, 256)
    tm_mm = _pick_tile(mp, 1024)
    tm_out = _pick_tile(mp, 512)

    st_p, st_s, mk_list, mv_list = [], [], [], []
    for l in range(depth):
        m_kv = _norm_matmul(mem2d, mem_norm_g, w_mem_kv, l, _pick_tile(bp * mem_tokens, 512), 1024)
        mk = m_kv[:, :mem_w].reshape(bp, mem_tokens, mem_w)
        mv = m_kv[:, mem_w:].reshape(bp, mem_tokens, mem_w)
        mk_list.append(mk.reshape(bp, mem_tokens, MEM_HEADS, mem_hd))
        mv_list.append(mv.reshape(bp, mem_tokens, MEM_HEADS, mem_hd))

        xn, k, v, ki, kb, kib, q_t, qi_t, wi_t, vb_t = _proj_attn(xp, norm_g, w_in_t, tabs_p, l, tm_p, True)
        z_rest = _matmul(xn, w_in_t, l, rest0, n_rest, tm_mm, 1024, True)
        br_att = _prompt_attention(qi_t, wi_t, kib, q_t, kb, vb_t, z_rest, bp, seq)
        br_lru, h_last, conv_new = _lru(z_rest, conv0, h0, *lru_w_args, l, bp, seq, _pick_tile(seq, 256), BF16)
        br_mem = _mem_attention(z_rest, mk, mv, bp, seq, _pick_tile(seq, 512), BF16)
        merged = _merge([br_att, br_lru, br_mem], z_rest, w_branch, l, tm_p, 1024)
        xp = _matmul(merged, w_out, l, 0, d, tm_out, 1024, False, residual=xp)
        st_p.append((k.reshape(bp, seq, KV_HEADS, HEAD_DIM), v.reshape(bp, seq, KV_HEADS, HEAD_DIM),
                     ki.reshape(bp, seq, IDX_DIM), conv_new, h_last.reshape(bp, lru_w)))

        xn, k, v, ki, kb, kib, q, qi, wi, vb = _proj_attn(xs, norm_g, w_in_t, tabs_s, l, ms, False)
        z_rest = _matmul(xn, w_in_t, l, rest0, n_rest, ms, 1024, True)
        qs = qi.reshape(ms * IDX_HEADS, IDX_DIM)
        ws = wi.reshape(ms * IDX_HEADS, 1)
        pad_rows = lambda a: jnp.pad(a.reshape(bs, t_new, a.shape[1]), ((0, 0), (0, LANES - t_new), (0, 0)))
        sc = _sample_index(page_table, qs, ws, pool_ik_t, l, bs, t_new, _pick_tile(n_pages, 2 * pages_per_step))
        bias, bias_new = _sample_select(sc, qs, ws, pad_rows(kib), bs, t_new, topk_s)
        br_att = _sample_attention(page_table, q, bias, bias_new, pad_rows(kb), pad_rows(vb), z_rest,
                                   pool_k, pool_v, l, bs, t_new, pages_per_step)
        br_lru, h_last, conv_new = _lru(z_rest, state_conv[l], h_s0[l], *lru_w_args, l, bs, t_new, t_new, F32)
        br_mem = _mem_attention(z_rest, mem_k_s[l], mem_v_s[l], bs, t_new, t_new, F32)
        merged = _merge([br_att, br_lru, br_mem], z_rest, w_branch, l, ms, 1024)
        xs = _matmul(merged, w_out, l, 0, d, ms, 1024, False, residual=xs)
        st_s.append((k.reshape(bs, t_new, KV_HEADS, HEAD_DIM), v.reshape(bs, t_new, KV_HEADS, HEAD_DIM),
                     ki.reshape(bs, t_new, IDX_DIM), conv_new, h_last.reshape(bs, lru_w)))

    g_f = final_norm_g.reshape(1, d)
    y_prompt = _rmsnorm(xp, g_f, _pick_tile(mp, 512)).reshape(bp, seq, d)
    y_sample = _rmsnorm(xs, g_f, ms).reshape(bs, t_new, d)
    k_p, v_p, ik_p, conv_p, h_p = [jnp.stack(t) for t in zip(*st_p)]
    k_s, v_s, ik_s, conv_s, h_s = [jnp.stack(t) for t in zip(*st_s)]
    return (y_prompt, y_sample, k_p, v_p, ik_p, conv_p, h_p, jnp.stack(mk_list), jnp.stack(mv_list),
            k_s, v_s, ik_s, conv_s, h_s)
```
